```python
import math
import jax, jax.numpy as jnp
from jax import lax
import numpy as np

D_MODEL = 1024
BATCH = 8
SEQ = 4096
DEPTH = 2

GRID_W = 64
Q_BLOCK = 128
EPS = 1e-6
ROPE_THETA = 500000.0
AXIAL_THETA = 10000.0

MLA_HEADS = 8
MLA_Q_RANK = 192
MLA_KV_RANK = 128
MLA_NOPE_DIM = 64
MLA_ROPE_DIM = 32
MLA_V_DIM = 64

DIFF_HEADS = 4
DIFF_DIM = 64
DIFF_V_DIM = 2 * DIFF_DIM
DIFF_ROT = DIFF_DIM // 4

GQA_HEADS = 8
GQA_KV_HEADS = 2
GQA_GROUP = GQA_HEADS // GQA_KV_HEADS
GQA_DIM = 128

FFN_HIDDEN = -(-8 * D_MODEL // (3 * 256)) * 256

EVEN_IN = MLA_Q_RANK + MLA_KV_RANK + MLA_ROPE_DIM + 3 * DIFF_HEADS * DIFF_V_DIM
EVEN_MIX = MLA_HEADS * MLA_V_DIM + DIFF_HEADS * DIFF_V_DIM
ODD_IN = (GQA_HEADS + 2 * GQA_KV_HEADS) * GQA_DIM
ODD_MIX = GQA_HEADS * GQA_DIM
N_EVEN = (DEPTH + 1) // 2
N_ODD = DEPTH // 2

kernel_name = "hybrid_mla_diff_gqa_axial_encoder"


def rmsnorm(x, g):
    xf = x.astype(jnp.float32)
    y = xf * lax.rsqrt(jnp.mean(xf * xf, axis=-1, keepdims=True) + EPS)
    return y.astype(x.dtype) * g


def rope_cos_sin(pos, dim, theta):
    inv = theta ** (-jnp.arange(0, dim, 2, dtype=jnp.float32) / dim)
    ang = pos.astype(jnp.float32)[:, None] * inv[None, :]
    return jnp.cos(ang), jnp.sin(ang)


def apply_rope(x, cos, sin):
    half = x.shape[-1] // 2
    xf = x.astype(jnp.float32)
    x1, x2 = xf[..., :half], xf[..., half:]
    bshape = (cos.shape[0],) + (1,) * (x.ndim - 3) + (half,)
    c, s = cos.reshape(bshape), sin.reshape(bshape)
    return jnp.concatenate([x1 * c - x2 * s, x2 * c + x1 * s], axis=-1).astype(x.dtype)


def partial_rope(x, cos, sin, rot):
    return jnp.concatenate([apply_rope(x[..., :rot], cos, sin), x[..., rot:]], axis=-1)


def axial_rope(x, row, col):
    half = x.shape[-1] // 2
    cr, sr = rope_cos_sin(row, half, AXIAL_THETA)
    cc, sc = rope_cos_sin(col, half, AXIAL_THETA)
    return jnp.concatenate([apply_rope(x[..., :half], cr, sr),
                            apply_rope(x[..., half:], cc, sc)], axis=-1)


def to_blocks(t):
    b, s = t.shape[:2]
    return jnp.moveaxis(t.reshape(b, s // Q_BLOCK, Q_BLOCK, *t.shape[2:]), 1, 0)


def from_blocks(t):
    nb, b, qb = t.shape[:3]
    return jnp.moveaxis(t, 0, 1).reshape(b, nb * qb, *t.shape[3:])


def grouped_attention(q, k, v):
    scale = q.shape[-1] ** -0.5

    def one_block(qb):
        s = jnp.einsum('bqhgd,bkhd->bhgqk', qb, k).astype(jnp.float32) * scale
        p = jax.nn.softmax(s, axis=-1).astype(v.dtype)
        return jnp.einsum('bhgqk,bkhd->bqhgd', p, v)

    return from_blocks(lax.map(one_block, to_blocks(q)))


def diff_attention(q, k, v, lam):
    scale = q.shape[-1] ** -0.5
    lam = lam.astype(jnp.float32)

    def one_block(qb):
        s = jnp.einsum('bqhcd,bkhcd->bhcqk', qb, k).astype(jnp.float32) * scale
        p = jax.nn.softmax(s, axis=-1)
        w = (p[:, :, 0] - lam * p[:, :, 1]).astype(v.dtype)
        return jnp.einsum('bhqk,bkhe->bqhe', w, v)

    return from_blocks(lax.map(one_block, to_blocks(q)))


def even_mixer(h, pos, layer_idx, w_in, q_norm, w_uq, kv_norm, w_ukv,
               lq1, lk1, lq2, lk2, subln, w_out):
    b, s, _ = h.shape
    proj = h @ w_in
    c_q, c_kv, k_rope, qkv_d = jnp.split(
        proj, [MLA_Q_RANK, MLA_Q_RANK + MLA_KV_RANK,
               MLA_Q_RANK + MLA_KV_RANK + MLA_ROPE_DIM], axis=-1)

    q = (rmsnorm(c_q, q_norm) @ w_uq).reshape(b, s, MLA_HEADS, MLA_NOPE_DIM + MLA_ROPE_DIM)
    kv = (rmsnorm(c_kv, kv_norm) @ w_ukv).reshape(b, s, MLA_HEADS, MLA_NOPE_DIM + MLA_V_DIM)
    cos, sin = rope_cos_sin(pos, MLA_ROPE_DIM, ROPE_THETA)
    q = jnp.concatenate([q[..., :MLA_NOPE_DIM], apply_rope(q[..., MLA_NOPE_DIM:], cos, sin)], axis=-1)
    k_r = apply_rope(k_rope, cos, sin)
    k = jnp.concatenate([kv[..., :MLA_NOPE_DIM],
                         jnp.broadcast_to(k_r[:, :, None, :], (b, s, MLA_HEADS, MLA_ROPE_DIM))], axis=-1)
    v = kv[..., MLA_NOPE_DIM:]
    o_mla = grouped_attention(q[:, :, :, None, :], k, v).reshape(b, s, MLA_HEADS * MLA_V_DIM)

    qd, kd, vd = jnp.split(qkv_d, 3, axis=-1)
    qd = qd.reshape(b, s, DIFF_HEADS, 2, DIFF_DIM)
    kd = kd.reshape(b, s, DIFF_HEADS, 2, DIFF_DIM)
    vd = vd.reshape(b, s, DIFF_HEADS, DIFF_V_DIM)
    cp, sp = rope_cos_sin(pos, DIFF_ROT, ROPE_THETA)
    qd = partial_rope(qd, cp, sp, DIFF_ROT)
    kd = partial_rope(kd, cp, sp, DIFF_ROT)
    lam_init = 0.8 - 0.6 * math.exp(-0.3 * layer_idx)
    lam = (jnp.exp(jnp.sum(lq1.astype(jnp.float32) * lk1.astype(jnp.float32)))
           - jnp.exp(jnp.sum(lq2.astype(jnp.float32) * lk2.astype(jnp.float32))) + lam_init)
    o_d = diff_attention(qd, kd, vd, lam)
    o_d = (rmsnorm(o_d, subln) * (1.0 - lam_init)).reshape(b, s, DIFF_HEADS * DIFF_V_DIM)

    return jnp.concatenate([o_mla, o_d], axis=-1) @ w_out


def odd_mixer(h, row, col, w_qkv, q_norm, k_norm, w_out):
    b, s, _ = h.shape
    proj = h @ w_qkv
    q, k, v = jnp.split(proj, [GQA_HEADS * GQA_DIM, (GQA_HEADS + GQA_KV_HEADS) * GQA_DIM], axis=-1)
    q = rmsnorm(q.reshape(b, s, GQA_HEADS, GQA_DIM), q_norm)
    k = rmsnorm(k.reshape(b, s, GQA_KV_HEADS, GQA_DIM), k_norm)
    v = v.reshape(b, s, GQA_KV_HEADS, GQA_DIM)
    q = axial_rope(q, row, col)
    k = axial_rope(k, row, col)
    o = grouped_attention(q.reshape(b, s, GQA_KV_HEADS, GQA_GROUP, GQA_DIM), k, v)
    return o.reshape(b, s, ODD_MIX) @ w_out


def swiglu(h, w_gate, w_up, w_down):
    return (jax.nn.silu(h @ w_gate) * (h @ w_up)) @ w_down


def setup_inputs(seed: int = 0) -> dict:
    key = jax.random.key(seed)
    ks = jax.random.split(key, 23)

    def w(k, shape, fan_in):
        return jax.random.normal(k, shape, jnp.float32) * (fan_in ** -0.5)

    def gain(k, shape):
        return 1.0 + 0.02 * jax.random.normal(k, shape, jnp.float32)

    def small(k, shape):
        return 0.1 * jax.random.normal(k, shape, jnp.float32)

    return {
        "x": jax.random.normal(ks[0], (BATCH, SEQ, D_MODEL), jnp.float32),
        "e_attn_norm": gain(ks[1], (N_EVEN, D_MODEL)),
        "e_w_in": w(ks[2], (N_EVEN, D_MODEL, EVEN_IN), D_MODEL),
        "e_q_norm": gain(ks[3], (N_EVEN, MLA_Q_RANK)),
        "e_w_uq": w(ks[4], (N_EVEN, MLA_Q_RANK, MLA_HEADS * (MLA_NOPE_DIM + MLA_ROPE_DIM)), MLA_Q_RANK),
        "e_kv_norm": gain(ks[5], (N_EVEN, MLA_KV_RANK)),
        "e_w_ukv": w(ks[6], (N_EVEN, MLA_KV_RANK, MLA_HEADS * (MLA_NOPE_DIM + MLA_V_DIM)), MLA_KV_RANK),
        "e_lambda_q1": small(ks[7], (N_EVEN, DIFF_DIM)),
        "e_lambda_k1": small(ks[8], (N_EVEN, DIFF_DIM)),
        "e_lambda_q2": small(ks[9], (N_EVEN, DIFF_DIM)),
        "e_lambda_k2": small(ks[10], (N_EVEN, DIFF_DIM)),
        "e_subln": gain(ks[11], (N_EVEN, DIFF_V_DIM)),
        "e_w_out": w(ks[12], (N_EVEN, EVEN_MIX, D_MODEL), EVEN_MIX),
        "o_attn_norm": gain(ks[13], (N_ODD, D_MODEL)),
        "o_w_qkv": w(ks[14], (N_ODD, D_MODEL, ODD_IN), D_MODEL),
        "o_q_norm": gain(ks[15], (N_ODD, GQA_DIM)),
        "o_k_norm": gain(ks[16], (N_ODD, GQA_DIM)),
        "o_w_out": w(ks[17], (N_ODD, ODD_MIX, D_MODEL), ODD_MIX),
        "ffn_norm": gain(ks[18], (DEPTH, D_MODEL)),
        "w_gate": w(ks[19], (DEPTH, D_MODEL, FFN_HIDDEN), D_MODEL),
        "w_up": w(ks[20], (DEPTH, D_MODEL, FFN_HIDDEN), D_MODEL),
        "w_down": w(ks[21], (DEPTH, FFN_HIDDEN, D_MODEL), FFN_HIDDEN),
        "final_norm": gain(ks[22], (D_MODEL,)),
    }


def reference(x, e_attn_norm, e_w_in, e_q_norm, e_w_uq, e_kv_norm, e_w_ukv,
              e_lambda_q1, e_lambda_k1, e_lambda_q2, e_lambda_k2, e_subln, e_w_out,
              o_attn_norm, o_w_qkv, o_q_norm, o_k_norm, o_w_out,
              ffn_norm, w_gate, w_up, w_down, final_norm):
    s = x.shape[1]
    rows = s // GRID_W
    pos = jnp.arange(s, dtype=jnp.int32)
    row = jnp.repeat(jnp.arange(rows, dtype=jnp.int32), GRID_W)
    col = jnp.tile(jnp.arange(GRID_W, dtype=jnp.int32), rows)

    h = x
    for i in range(DEPTH):
        j = i // 2
        if i % 2 == 0:
            h = h + even_mixer(rmsnorm(h, e_attn_norm[j]), pos, i, e_w_in[j], e_q_norm[j],
                               e_w_uq[j], e_kv_norm[j], e_w_ukv[j], e_lambda_q1[j],
                               e_lambda_k1[j], e_lambda_q2[j], e_lambda_k2[j],
                               e_subln[j], e_w_out[j])
        else:
            h = h + odd_mixer(rmsnorm(h, o_attn_norm[j]), row, col, o_w_qkv[j],
                              o_q_norm[j], o_k_norm[j], o_w_out[j])
        h = h + swiglu(rmsnorm(h, ffn_norm[i]), w_gate[i], w_up[i], w_down[i])
    return rmsnorm(h, final_norm)
```

```python
import functools
import math

import jax
import jax.numpy as jnp
import numpy as np
from jax import lax
from jax.experimental import pallas as pl
from jax.experimental.pallas import tpu as pltpu

F32 = jnp.float32
BF16 = jnp.bfloat16

D_MODEL = 1024
SEQ = 4096
GRID_W = 64
EPS = 1e-6
ROPE_THETA = 500000.0
AXIAL_THETA = 10000.0

MLA_HEADS = 8
MLA_Q_RANK = 192
MLA_KV_RANK = 128
MLA_NOPE = 64
MLA_ROPE = 32
MLA_V = 64
DIFF_HEADS = 4
DIFF_DIM = 64
DIFF_ROT = 16
GQA_HEADS = 8
GQA_KV_HEADS = 2
GQA_GROUP = 4
FFN_HIDDEN = 2816

LANES = 128
HALF = LANES // 2
VMEM_LIMIT = 56 * 1024 * 1024

ROW_TILE = 512
ATT_ROWS = 512
KV_TILE = 512
NEG_INIT = -1e30


def _diff_slab_perm():
    first = list(range(0, 8)) + list(range(16, 40))
    second = list(range(8, 16)) + list(range(40, 64))
    perm = []
    for blk in range(4):
        comp, half = blk % 2, blk // 2
        dims = first if half == 0 else second
        perm += [comp * DIFF_DIM + d for d in dims]
    return np.array(perm, np.int32)


def _mla_slot_lanes():
    lanes = np.zeros(96, np.int32)
    lanes[0:48] = np.arange(0, 48)
    lanes[48:64] = np.arange(64, 80)
    lanes[64:80] = np.arange(48, 64)
    lanes[80:96] = np.arange(112, 128)
    return lanes


def _axial_perm():
    return np.concatenate([np.arange(0, 32), np.arange(64, 96), np.arange(32, 64), np.arange(96, 128)]).astype(np.int32)


def _rope_tables():
    pos = jnp.arange(SEQ, dtype=jnp.int32).astype(F32)
    row = (jnp.arange(SEQ, dtype=jnp.int32) // GRID_W).astype(F32)
    col = (jnp.arange(SEQ, dtype=jnp.int32) % GRID_W).astype(F32)

    def cs(p, dim, theta):
        inv = theta ** (-jnp.arange(0, dim, 2, dtype=F32) / dim)
        ang = p[:, None] * inv[None, :]
        return jnp.cos(ang), jnp.sin(ang)

    ones = lambda n: jnp.ones((SEQ, n), F32)
    zeros = lambda n: jnp.zeros((SEQ, n), F32)

    c, s = cs(pos, DIFF_ROT, ROPE_THETA)
    cblk = jnp.concatenate([c, ones(24)], axis=1)
    sblk = jnp.concatenate([s, zeros(24)], axis=1)
    diff_cos = jnp.concatenate([cblk] * 4, axis=1)
    diff_sin = jnp.concatenate([-sblk, -sblk, sblk, sblk], axis=1)

    c, s = cs(pos, MLA_ROPE, ROPE_THETA)
    mla_cos = jnp.concatenate([ones(48), c, ones(48), c], axis=1)
    mla_sin = jnp.concatenate([zeros(48), -s, zeros(48), s], axis=1)

    cr, sr = cs(row, HALF, AXIAL_THETA)
    cc, sc = cs(col, HALF, AXIAL_THETA)
    ax_cos = jnp.concatenate([cr, cc, cr, cc], axis=1)
    ax_sin = jnp.concatenate([-sr, -sc, sr, sc], axis=1)
    return diff_cos, diff_sin, mla_cos, mla_sin, ax_cos, ax_sin


def _rms(x, gain, n):
    ms = jnp.sum(x * x, axis=-1, keepdims=True) * (1.0 / n)
    return x * lax.rsqrt(ms + EPS) * gain


def _rope(x, cos, sin):
    return x * cos + pltpu.roll(x, HALF, 1) * sin


def _dot(a, b):
    return jnp.dot(a, b, preferred_element_type=F32)


def _even_proj_kernel(x_ref, g_ref, wbig_ref, qn_ref, wuq_ref, kvn_ref, wukv_ref,
                      dcos_ref, dsin_ref, mcos_ref, msin_ref,
                      qm_ref, km_ref, vm_ref, qd_ref, kd_ref, vd_ref, *, q_scale_mla, q_scale_diff):
    x = x_ref[...]
    xn = _rms(x, g_ref[...], D_MODEL).astype(BF16)
    big = _dot(xn, wbig_ref[...])
    dcos, dsin = dcos_ref[...], dsin_ref[...]
    mcos, msin = mcos_ref[...], msin_ref[...]

    for h in range(DIFF_HEADS):
        sl = slice(h * LANES, (h + 1) * LANES)
        qd_ref[:, sl] = (_rope(big[:, sl], dcos, dsin) * q_scale_diff).astype(BF16)
        kd_ref[:, sl] = _rope(big[:, 512 + h * LANES:512 + (h + 1) * LANES], dcos, dsin).astype(BF16)
    vd_ref[...] = big[:, 1024:1536].astype(BF16)

    cq = big[:, 1536:1792]
    cqn = _rms(cq, qn_ref[...], MLA_Q_RANK).astype(BF16)
    q = _dot(cqn, wuq_ref[...])
    ckv = big[:, 1792:1920]
    ckvn = _rms(ckv, kvn_ref[...], MLA_KV_RANK).astype(BF16)
    kv = _dot(ckvn, wukv_ref[...])
    kr = _rope(big[:, 1920:2048], mcos, msin)
    for h in range(MLA_HEADS):
        sl = slice(h * LANES, (h + 1) * LANES)
        qm_ref[:, sl] = (_rope(q[:, sl], mcos, msin) * q_scale_mla).astype(BF16)
        km_ref[:, sl] = (kv[:, sl] + kr).astype(BF16)
    vm_ref[...] = kv[:, 1024:1536].astype(BF16)


def _even_proj(x2, g, wbig, qn, wuq, kvn, wukv, tabs, q_scale_mla, q_scale_diff):
    n = x2.shape[0]
    tm = ROW_TILE
    ns = SEQ // tm
    row = lambda w: pl.BlockSpec((tm, w), lambda i: (i, 0))
    full = lambda a: pl.BlockSpec(a.shape, lambda i: (0, 0))
    tab = pl.BlockSpec((tm, LANES), lambda i: (i % ns, 0))
    dcos, dsin, mcos, msin = tabs
    out_w = (1024, 1024, 512, 512, 512, 512)
    return pl.pallas_call(
        functools.partial(_even_proj_kernel, q_scale_mla=q_scale_mla, q_scale_diff=q_scale_diff),
        grid=(n // tm,),
        in_specs=[row(D_MODEL), full(g), full(wbig), full(qn), full(wuq), full(kvn), full(wukv), tab, tab, tab, tab],
        out_specs=[row(w) for w in out_w],
        out_shape=[jax.ShapeDtypeStruct((n, w), BF16) for w in out_w],
        compiler_params=pltpu.CompilerParams(dimension_semantics=("parallel",), vmem_limit_bytes=VMEM_LIMIT),
        name="even_proj",
    )(x2, g, wbig, qn, wuq, kvn, wukv, dcos, dsin, mcos, msin)


def _flash_rows(q, k_ref, v_ref, kcol, vcol):
    rows = q.shape[0]
    n_kv = k_ref.shape[0] // KV_TILE

    def body(j, carry):
        m, l, acc = carry
        start = pl.multiple_of(j * KV_TILE, KV_TILE)
        ks = k_ref[pl.ds(start, KV_TILE), kcol:kcol + LANES]
        vs = v_ref[pl.ds(start, KV_TILE), vcol:vcol + LANES]
        s = lax.dot_general(q, ks, (((1,), (1,)), ((), ())), preferred_element_type=F32)
        m_new = jnp.maximum(m, jnp.max(s, axis=-1, keepdims=True))
        alpha = jnp.exp2(m - m_new)
        p = jnp.exp2(s - m_new)
        l_new = alpha * l + jnp.sum(p, axis=-1, keepdims=True)
        acc_new = alpha * acc + _dot(p.astype(BF16), vs)
        return m_new, l_new, acc_new

    init = (jnp.full((rows, 1), NEG_INIT, F32), jnp.zeros((rows, 1), F32), jnp.zeros((rows, LANES), F32))
    _, l, acc = lax.fori_loop(0, n_kv, body, init)
    return acc, l


def _mla_attn_kernel(q_ref, k_ref, v_ref, o_ref):
    acc0, l0 = _flash_rows(q_ref[:, 0:LANES], k_ref, v_ref, 0, 0)
    acc1, l1 = _flash_rows(q_ref[:, LANES:2 * LANES], k_ref, v_ref, LANES, 0)
    lane = lax.broadcasted_iota(jnp.int32, acc0.shape, 1)
    o = jnp.where(lane < MLA_V, acc0 * (1.0 / l0), acc1 * (1.0 / l1))
    o_ref[...] = o.astype(BF16)


def _diff_attn_kernel(q_ref, k_ref, v_ref, lq1_ref, lk1_ref, lq2_ref, lk2_ref, sub_ref, o_ref, *, lam_init):
    q = q_ref[...]
    tq = q.shape[0]
    lane = lax.broadcasted_iota(jnp.int32, q.shape, 1)
    comp0 = (lane % HALF) < (HALF // 2)
    zero = jnp.zeros_like(q)
    q2 = jnp.concatenate([jnp.where(comp0, q, zero), jnp.where(comp0, zero, q)], axis=0)
    acc, l = _flash_rows(q2, k_ref, v_ref, 0, 0)
    o = acc * (1.0 / l)
    lam = (jnp.exp(jnp.sum(lq1_ref[...] * lk1_ref[...], axis=-1, keepdims=True))
           - jnp.exp(jnp.sum(lq2_ref[...] * lk2_ref[...], axis=-1, keepdims=True)) + lam_init)
    od = o[:tq] - lam * o[tq:]
    od = _rms(od, sub_ref[...], LANES) * (1.0 - lam_init)
    o_ref[...] = od.astype(BF16)


def _gqa_attn_kernel(q_ref, k_ref, v_ref, o_ref):
    tq = q_ref.shape[0]
    q = jnp.concatenate([q_ref[:, g * LANES:(g + 1) * LANES] for g in range(GQA_GROUP)], axis=0)
    acc, l = _flash_rows(q, k_ref, v_ref, 0, 0)
    o = acc * (1.0 / l)
    for g in range(GQA_GROUP):
        o_ref[:, g * LANES:(g + 1) * LANES] = o[g * tq:(g + 1) * tq].astype(BF16)


def _attention(body, q, k, v, extra, *, q_w, k_w, v_w, o_w, tq, n_groups, name):
    n = q.shape[0]
    b = n // SEQ
    nq = SEQ // tq
    in_specs = [
        pl.BlockSpec((tq, q_w), lambda bi, g, i: (bi * nq + i, g)),
        pl.BlockSpec((SEQ, k_w), lambda bi, g, i: (bi, g)),
        pl.BlockSpec((SEQ, v_w), lambda bi, g, i: (bi, g)),
    ] + [pl.BlockSpec(a.shape, lambda bi, g, i: (0, 0)) for a in extra]
    return pl.pallas_call(
        body,
        grid=(b, n_groups, nq),
        in_specs=in_specs,
        out_specs=pl.BlockSpec((tq, o_w), lambda bi, g, i: (bi * nq + i, g)),
        out_shape=jax.ShapeDtypeStruct((n, n_groups * o_w), BF16),
        compiler_params=pltpu.CompilerParams(
            dimension_semantics=("parallel", "parallel", "parallel"), vmem_limit_bytes=VMEM_LIMIT),
        name=name,
    )(q, k, v, *extra)


def _post_kernel(*refs, n_mix, final):
    h_ref = refs[0]
    o_refs = refs[1:1 + n_mix]
    w_refs = refs[1 + n_mix:1 + 2 * n_mix]
    g_ref, wg_ref, wu_ref, wd_ref = refs[1 + 2 * n_mix:5 + 2 * n_mix]
    rest = refs[5 + 2 * n_mix:]
    if final:
        fg_ref, out_ref = rest
    else:
        (out_ref,) = rest
    h = h_ref[...]
    for o_ref, w_ref in zip(o_refs, w_refs):
        h = h + _dot(o_ref[...], w_ref[...])
    hn = _rms(h, g_ref[...], D_MODEL).astype(BF16)
    gate = _dot(hn, wg_ref[...])
    up = _dot(hn, wu_ref[...])
    act = (gate * (1.0 / (1.0 + jnp.exp(-gate))) * up).astype(BF16)
    h = h + _dot(act, wd_ref[...])
    if final:
        h = _rms(h, fg_ref[...], D_MODEL)
    out_ref[...] = h


def _post(h2, mixes, w_outs, g, wg, wu, wd, final_gain=None):
    n = h2.shape[0]
    tm = ROW_TILE
    row = lambda w: pl.BlockSpec((tm, w), lambda i: (i, 0))
    const = lambda a: pl.BlockSpec(a.shape, lambda i: (0, 0), pipeline_mode=pl.Buffered(1))
    final = final_gain is not None
    args = [h2, *mixes, *w_outs, g, wg, wu, wd] + ([final_gain] if final else [])
    in_specs = ([row(D_MODEL)] + [row(m.shape[1]) for m in mixes] + [const(w) for w in w_outs]
                + [const(g), const(wg), const(wu), const(wd)] + ([const(final_gain)] if final else []))
    return pl.pallas_call(
        functools.partial(_post_kernel, n_mix=len(mixes), final=final),
        grid=(n // tm,),
        in_specs=in_specs,
        out_specs=row(D_MODEL),
        out_shape=jax.ShapeDtypeStruct((n, D_MODEL), F32),
        compiler_params=pltpu.CompilerParams(dimension_semantics=("parallel",), vmem_limit_bytes=VMEM_LIMIT),
        name="post_final" if final else "post",
    )(*args)


def _odd_proj_kernel(h_ref, g_ref, w_ref, qn_ref, kn_ref, cos_ref, sin_ref, q_ref, k_ref, v_ref, *, q_scale):
    hn = _rms(h_ref[...], g_ref[...], D_MODEL).astype(BF16)
    proj = _dot(hn, w_ref[...])
    cos, sin = cos_ref[...], sin_ref[...]
    qn, kn = qn_ref[...], kn_ref[...]
    for h in range(GQA_HEADS):
        sl = slice(h * LANES, (h + 1) * LANES)
        q_ref[:, sl] = (_rope(_rms(proj[:, sl], qn, LANES), cos, sin) * q_scale).astype(BF16)
    for h in range(GQA_KV_HEADS):
        sl = slice(h * LANES, (h + 1) * LANES)
        k_ref[:, sl] = _rope(_rms(proj[:, 1024 + h * LANES:1024 + (h + 1) * LANES], kn, LANES), cos, sin).astype(BF16)
    v_ref[...] = proj[:, 1280:1536].astype(BF16)


def _odd_proj(h2, g, w, qn, kn, cos, sin, q_scale):
    n = h2.shape[0]
    tm = ROW_TILE
    ns = SEQ // tm
    row = lambda w_: pl.BlockSpec((tm, w_), lambda i: (i, 0))
    full = lambda a: pl.BlockSpec(a.shape, lambda i: (0, 0))
    tab = pl.BlockSpec((tm, LANES), lambda i: (i % ns, 0))
    out_w = (1024, 256, 256)
    return pl.pallas_call(
        functools.partial(_odd_proj_kernel, q_scale=q_scale),
        grid=(n // tm,),
        in_specs=[row(D_MODEL), full(g), full(w), full(qn), full(kn), tab, tab],
        out_specs=[row(w_) for w_ in out_w],
        out_shape=[jax.ShapeDtypeStruct((n, w_), BF16) for w_ in out_w],
        compiler_params=pltpu.CompilerParams(dimension_semantics=("parallel",), vmem_limit_bytes=VMEM_LIMIT),
        name="odd_proj",
    )(h2, g, w, qn, kn, cos, sin)


def _prep_even_weights(w_in, q_norm, w_uq, kv_norm, w_ukv):
    o_kv = MLA_Q_RANK
    o_kr = o_kv + MLA_KV_RANK
    o_d = o_kr + MLA_ROPE
    w_cq, w_ckv, w_kr = w_in[:, :o_kv], w_in[:, o_kv:o_kr], w_in[:, o_kr:o_d]
    w_qd, w_kd, w_vd = w_in[:, o_d:o_d + 512], w_in[:, o_d + 512:o_d + 1024], w_in[:, o_d + 1024:o_d + 1536]
    perm = _diff_slab_perm()
    cols = np.concatenate([h * LANES + perm for h in range(DIFF_HEADS)])
    w_qd, w_kd = w_qd[:, cols], w_kd[:, cols]
    slot = _mla_slot_lanes()
    w_kr_slot = jnp.zeros((D_MODEL, LANES), F32).at[:, slot[MLA_NOPE:]].set(w_kr)
    wbig = jnp.concatenate(
        [w_qd, w_kd, w_vd, w_cq, jnp.zeros((D_MODEL, 256 - MLA_Q_RANK), F32), w_ckv, w_kr_slot], axis=1).astype(BF16)

    qn = jnp.concatenate([q_norm, jnp.zeros((256 - MLA_Q_RANK,), F32)])[None, :]
    uq = w_uq.reshape(MLA_Q_RANK, MLA_HEADS, MLA_NOPE + MLA_ROPE)
    wuq = jnp.zeros((256, MLA_HEADS, LANES), F32).at[:MLA_Q_RANK, :, slot].set(uq).reshape(256, MLA_HEADS * LANES).astype(BF16)

    ukv = w_ukv.reshape(MLA_KV_RANK, MLA_HEADS, MLA_NOPE + MLA_V)
    wk = jnp.zeros((MLA_KV_RANK, MLA_HEADS, LANES), F32).at[:, :, slot[:MLA_NOPE]].set(ukv[:, :, :MLA_NOPE])
    wv = ukv[:, :, MLA_NOPE:]
    wukv = jnp.concatenate([wk.reshape(MLA_KV_RANK, -1), wv.reshape(MLA_KV_RANK, -1)], axis=1).astype(BF16)
    return wbig, qn, wuq, kv_norm[None, :], wukv


def _prep_odd_weights(w_qkv, q_norm, k_norm):
    perm = _axial_perm()
    qcols = np.concatenate([h * LANES + perm for h in range(GQA_HEADS)])
    kcols = 1024 + np.concatenate([h * LANES + perm for h in range(GQA_KV_HEADS)])
    cols = np.concatenate([qcols, kcols, np.arange(1280, 1536)])
    return w_qkv[:, cols].astype(BF16), q_norm[perm][None, :], k_norm[perm][None, :]


def kernel(x, e_attn_norm, e_w_in, e_q_norm, e_w_uq, e_kv_norm, e_w_ukv, e_lambda_q1, e_lambda_k1, e_lambda_q2, e_lambda_k2, e_subln, e_w_out, o_attn_norm, o_w_qkv, o_q_norm, o_k_norm, o_w_out, ffn_norm, w_gate, w_up, w_down, final_norm):
    b, s, d = x.shape
    assert (s, d) == (SEQ, D_MODEL)
    n = b * s
    log2e = math.log2(math.e)
    dcos, dsin, mcos, msin, acos, asin = _rope_tables()
    h = x.reshape(n, d)

    wbig, qn, wuq, kvn, wukv = _prep_even_weights(e_w_in[0], e_q_norm[0], e_w_uq[0], e_kv_norm[0], e_w_ukv[0])
    qm, km, vm, qd, kd, vd = _even_proj(
        h, e_attn_norm[0][None, :], wbig, qn, wuq, kvn, wukv, (dcos, dsin, mcos, msin),
        q_scale_mla=(MLA_NOPE + MLA_ROPE) ** -0.5 * log2e, q_scale_diff=DIFF_DIM ** -0.5 * log2e)
    o_mla = _attention(_mla_attn_kernel, qm, km, vm, [], q_w=2 * LANES, k_w=2 * LANES, v_w=LANES, o_w=LANES,
                       tq=ATT_ROWS, n_groups=MLA_HEADS // 2, name="mla_attn")
    lam_init = 0.8 - 0.6 * math.exp(-0.3 * 0)
    lam_args = [e_lambda_q1[0][None, :], e_lambda_k1[0][None, :], e_lambda_q2[0][None, :], e_lambda_k2[0][None, :],
                e_subln[0][None, :]]
    o_diff = _attention(functools.partial(_diff_attn_kernel, lam_init=lam_init), qd, kd, vd, lam_args,
                        q_w=LANES, k_w=LANES, v_w=LANES, o_w=LANES, tq=ATT_ROWS // 2, n_groups=DIFF_HEADS,
                        name="diff_attn")
    w_out = e_w_out[0].astype(BF16)
    h = _post(h, [o_mla, o_diff], [w_out[:512], w_out[512:]], ffn_norm[0][None, :],
              w_gate[0].astype(BF16), w_up[0].astype(BF16), w_down[0].astype(BF16))

    wqkv, oqn, okn = _prep_odd_weights(o_w_qkv[0], o_q_norm[0], o_k_norm[0])
    q, k, v = _odd_proj(h, o_attn_norm[0][None, :], wqkv, oqn, okn, acos, asin, q_scale=LANES ** -0.5 * log2e)
    o = _attention(_gqa_attn_kernel, q, k, v, [], q_w=GQA_GROUP * LANES, k_w=LANES, v_w=LANES, o_w=GQA_GROUP * LANES,
                   tq=ATT_ROWS // GQA_GROUP, n_groups=GQA_KV_HEADS, name="gqa_attn")
    out = _post(h, [o], [o_w_out[0].astype(BF16)], ffn_norm[1][None, :],
                w_gate[1].astype(BF16), w_up[1].astype(BF16), w_down[1].astype(BF16), final_gain=final_norm[None, :])
    return out.reshape(b, s, d)
```

```python
import functools
import math

import jax
import jax.numpy as jnp
from jax import lax
from jax.experimental import pallas as pl
from jax.experimental.pallas import tpu as pltpu

F32 = jnp.float32
BF16 = jnp.bfloat16

D_MODEL = 1024
SEQ = 4096
GRID_W = 64
EPS = 1e-6
ROPE_THETA = 500000.0
AXIAL_THETA = 10000.0

MLA_HEADS = 8
MLA_Q_RANK = 192
MLA_KV_RANK = 128
MLA_NOPE = 64
MLA_ROPE = 32
MLA_V = 64
DIFF_HEADS = 4
DIFF_DIM = 64
DIFF_ROT = 16
GQA_HEADS = 8
GQA_KV_HEADS = 2
GQA_GROUP = 4
FFN_HIDDEN = 2816

LANES = 128
HALF = LANES // 2
VMEM_LIMIT = 56 * 1024 * 1024

ROW_TILE = 512
ATT_ROWS = 1024
KV_TILE = 512
NEG_INIT = -1e30


def _cat_slices(a, pieces):
    parts = []
    for p in pieces:
        if isinstance(p, int):
            parts.append(jnp.zeros(a.shape[:-1] + (p,), a.dtype))
        else:
            parts.append(a[..., p[0]:p[1]])
    return jnp.concatenate(parts, axis=-1)


_DIFF_PIECES = ((0, 8), (16, 40), (64, 72), (80, 104), (8, 16), (40, 64), (72, 80), (104, 128))
_MLA_QK_PIECES = ((0, 48), (64, 80), (48, 64), 32, (80, 96))
_MLA_KNOPE_PIECES = ((0, 48), 16, (48, 64), 48)
_MLA_KROPE_PIECES = (48, (0, 16), 48, (16, 32))
_AXIAL_PIECES = ((0, 32), (64, 96), (32, 64), (96, 128))


def _rope_tables():
    pos = jnp.arange(SEQ, dtype=jnp.int32).astype(F32)
    row = (jnp.arange(SEQ, dtype=jnp.int32) // GRID_W).astype(F32)
    col = (jnp.arange(SEQ, dtype=jnp.int32) % GRID_W).astype(F32)

    def cs(p, dim, theta):
        inv = theta ** (-jnp.arange(0, dim, 2, dtype=F32) / dim)
        ang = p[:, None] * inv[None, :]
        return jnp.cos(ang), jnp.sin(ang)

    ones = lambda n: jnp.ones((SEQ, n), F32)
    zeros = lambda n: jnp.zeros((SEQ, n), F32)

    c, s = cs(pos, DIFF_ROT, ROPE_THETA)
    cblk = jnp.concatenate([c, ones(24)], axis=1)
    sblk = jnp.concatenate([s, zeros(24)], axis=1)
    diff_cos = jnp.concatenate([cblk] * 4, axis=1)
    diff_sin = jnp.concatenate([-sblk, -sblk, sblk, sblk], axis=1)

    c, s = cs(pos, MLA_ROPE, ROPE_THETA)
    mla_cos = jnp.concatenate([ones(48), c, ones(48), c], axis=1)
    mla_sin = jnp.concatenate([zeros(48), -s, zeros(48), s], axis=1)

    cr, sr = cs(row, HALF, AXIAL_THETA)
    cc, sc = cs(col, HALF, AXIAL_THETA)
    ax_cos = jnp.concatenate([cr, cc, cr, cc], axis=1)
    ax_sin = jnp.concatenate([-sr, -sc, sr, sc], axis=1)
    return diff_cos, diff_sin, mla_cos, mla_sin, ax_cos, ax_sin


def _rms(x, gain, n):
    ms = jnp.sum(x * x, axis=-1, keepdims=True) * (1.0 / n)
    return x * lax.rsqrt(ms + EPS) * gain


def _rope(x, cos, sin):
    return x * cos + pltpu.roll(x, HALF, 1) * sin


def _dot(a, b):
    return jnp.dot(a, b, preferred_element_type=F32)


def _even_proj_kernel(x_ref, g_ref, wbig_ref, qn_ref, wuq_ref, kvn_ref, wukv_ref,
                      dcos_ref, dsin_ref, mcos_ref, msin_ref,
                      qm_ref, km_ref, vm_ref, qd_ref, kd_ref, vd_ref, *, q_scale_mla, q_scale_diff):
    x = x_ref[...]
    xn = _rms(x, g_ref[...], D_MODEL).astype(BF16)
    big = _dot(xn, wbig_ref[...])
    dcos, dsin = dcos_ref[...], dsin_ref[...]
    mcos, msin = mcos_ref[...], msin_ref[...]

    for h in range(DIFF_HEADS):
        sl = slice(h * LANES, (h + 1) * LANES)
        qd_ref[:, sl] = (_rope(big[:, sl], dcos, dsin) * q_scale_diff).astype(BF16)
        kd_ref[:, sl] = _rope(big[:, 512 + h * LANES:512 + (h + 1) * LANES], dcos, dsin).astype(BF16)
    vd_ref[...] = big[:, 1024:1536].astype(BF16)

    cq = big[:, 1536:1792]
    cqn = _rms(cq, qn_ref[...], MLA_Q_RANK).astype(BF16)
    q = _dot(cqn, wuq_ref[...])
    ckv = big[:, 1792:1920]
    ckvn = _rms(ckv, kvn_ref[...], MLA_KV_RANK).astype(BF16)
    kv = _dot(ckvn, wukv_ref[...])
    kr = _rope(big[:, 1920:2048], mcos, msin)
    for h in range(MLA_HEADS):
        sl = slice(h * LANES, (h + 1) * LANES)
        qm_ref[:, sl] = (_rope(q[:, sl], mcos, msin) * q_scale_mla).astype(BF16)
        km_ref[:, sl] = (kv[:, sl] + kr).astype(BF16)
    vm_ref[...] = kv[:, 1024:1536].astype(BF16)


def _even_proj(x2, g, wbig, qn, wuq, kvn, wukv, tabs, q_scale_mla, q_scale_diff):
    n = x2.shape[0]
    tm = ROW_TILE
    ns = SEQ // tm
    row = lambda w: pl.BlockSpec((tm, w), lambda i: (i, 0))
    full = lambda a: pl.BlockSpec(a.shape, lambda i: (0, 0))
    tab = pl.BlockSpec((tm, LANES), lambda i: (i % ns, 0))
    dcos, dsin, mcos, msin = tabs
    out_w = (1024, 1024, 512, 512, 512, 512)
    return pl.pallas_call(
        functools.partial(_even_proj_kernel, q_scale_mla=q_scale_mla, q_scale_diff=q_scale_diff),
        grid=(n // tm,),
        in_specs=[row(D_MODEL), full(g), full(wbig), full(qn), full(wuq), full(kvn), full(wukv), tab, tab, tab, tab],
        out_specs=[row(w) for w in out_w],
        out_shape=[jax.ShapeDtypeStruct((n, w), BF16) for w in out_w],
        compiler_params=pltpu.CompilerParams(dimension_semantics=("parallel",), vmem_limit_bytes=VMEM_LIMIT),
        name="even_proj",
    )(x2, g, wbig, qn, wuq, kvn, wukv, dcos, dsin, mcos, msin)


def _flash_rows(q, k_ref, v_ref, kcol, vcol):
    rows = q.shape[0]
    m = jnp.full((rows, 1), NEG_INIT, F32)
    l = jnp.zeros((rows, 1), F32)
    acc = jnp.zeros((rows, LANES), F32)
    for start in range(0, k_ref.shape[0], KV_TILE):
        ks = k_ref[start:start + KV_TILE, kcol:kcol + LANES]
        vs = v_ref[start:start + KV_TILE, vcol:vcol + LANES]
        s = lax.dot_general(q, ks, (((1,), (1,)), ((), ())), preferred_element_type=F32)
        m_new = jnp.maximum(m, jnp.max(s, axis=-1, keepdims=True))
        alpha = jnp.exp2(m - m_new)
        p = jnp.exp2(s - m_new)
        l = alpha * l + jnp.sum(p, axis=-1, keepdims=True)
        acc = alpha * acc + _dot(p.astype(BF16), vs)
        m = m_new
    return acc, l


def _mla_attn_kernel(q_ref, k_ref, v_ref, o_ref):
    acc0, l0 = _flash_rows(q_ref[:, 0:LANES], k_ref, v_ref, 0, 0)
    acc1, l1 = _flash_rows(q_ref[:, LANES:2 * LANES], k_ref, v_ref, LANES, 0)
    lane = lax.broadcasted_iota(jnp.int32, acc0.shape, 1)
    o = jnp.where(lane < MLA_V, acc0 * (1.0 / l0), acc1 * (1.0 / l1))
    o_ref[...] = o.astype(BF16)


def _diff_attn_kernel(q_ref, k_ref, v_ref, lq1_ref, lk1_ref, lq2_ref, lk2_ref, sub_ref, o_ref, *, lam_init):
    q = q_ref[...]
    tq = q.shape[0]
    lane = lax.broadcasted_iota(jnp.int32, q.shape, 1)
    comp0 = (lane % HALF) < (HALF // 2)
    zero = jnp.zeros_like(q)
    q2 = jnp.concatenate([jnp.where(comp0, q, zero), jnp.where(comp0, zero, q)], axis=0)
    acc, l = _flash_rows(q2, k_ref, v_ref, 0, 0)
    o = acc * (1.0 / l)
    lam = (jnp.exp(jnp.sum(lq1_ref[...] * lk1_ref[...], axis=-1, keepdims=True))
           - jnp.exp(jnp.sum(lq2_ref[...] * lk2_ref[...], axis=-1, keepdims=True)) + lam_init)
    od = o[:tq] - lam * o[tq:]
    od = _rms(od, sub_ref[...], LANES) * (1.0 - lam_init)
    o_ref[...] = od.astype(BF16)


def _gqa_attn_kernel(q_ref, k_ref, v_ref, o_ref):
    tq = q_ref.shape[0]
    q = jnp.concatenate([q_ref[:, g * LANES:(g + 1) * LANES] for g in range(GQA_GROUP)], axis=0)
    acc, l = _flash_rows(q, k_ref, v_ref, 0, 0)
    o = acc * (1.0 / l)
    for g in range(GQA_GROUP):
        o_ref[:, g * LANES:(g + 1) * LANES] = o[g * tq:(g + 1) * tq].astype(BF16)


def _attention(body, q, k, v, extra, *, q_w, k_w, v_w, o_w, tq, n_groups, name):
    n = q.shape[0]
    b = n // SEQ
    nq = SEQ // tq
    in_specs = [
        pl.BlockSpec((tq, q_w), lambda bi, g, i: (bi * nq + i, g)),
        pl.BlockSpec((SEQ, k_w), lambda bi, g, i: (bi, g)),
        pl.BlockSpec((SEQ, v_w), lambda bi, g, i: (bi, g)),
    ] + [pl.BlockSpec(a.shape, lambda bi, g, i: (0, 0)) for a in extra]
    return pl.pallas_call(
        body,
        grid=(b, n_groups, nq),
        in_specs=in_specs,
        out_specs=pl.BlockSpec((tq, o_w), lambda bi, g, i: (bi * nq + i, g)),
        out_shape=jax.ShapeDtypeStruct((n, n_groups * o_w), BF16),
        compiler_params=pltpu.CompilerParams(
            dimension_semantics=("parallel", "parallel", "parallel"), vmem_limit_bytes=VMEM_LIMIT),
        name=name,
    )(q, k, v, *extra)


def _post_kernel(*refs, n_mix, final):
    h_ref = refs[0]
    o_refs = refs[1:1 + n_mix]
    w_refs = refs[1 + n_mix:1 + 2 * n_mix]
    g_ref, wg_ref, wu_ref, wd_ref = refs[1 + 2 * n_mix:5 + 2 * n_mix]
    rest = refs[5 + 2 * n_mix:]
    if final:
        fg_ref, out_ref = rest
    else:
        (out_ref,) = rest
    h = h_ref[...]
    for o_ref, w_ref in zip(o_refs, w_refs):
        h = h + _dot(o_ref[...], w_ref[...])
    hn = _rms(h, g_ref[...], D_MODEL).astype(BF16)
    gate = _dot(hn, wg_ref[...])
    up = _dot(hn, wu_ref[...])
    act = (gate * (1.0 / (1.0 + jnp.exp(-gate))) * up).astype(BF16)
    h = h + _dot(act, wd_ref[...])
    if final:
        h = _rms(h, fg_ref[...], D_MODEL)
    out_ref[...] = h


def _post(h2, mixes, w_outs, g, wg, wu, wd, final_gain=None):
    n = h2.shape[0]
    tm = ROW_TILE
    row = lambda w: pl.BlockSpec((tm, w), lambda i: (i, 0))
    const = lambda a: pl.BlockSpec(a.shape, lambda i: (0, 0), pipeline_mode=pl.Buffered(1))
    final = final_gain is not None
    args = [h2, *mixes, *w_outs, g, wg, wu, wd] + ([final_gain] if final else [])
    in_specs = ([row(D_MODEL)] + [row(m.shape[1]) for m in mixes] + [const(w) for w in w_outs]
                + [const(g), const(wg), const(wu), const(wd)] + ([const(final_gain)] if final else []))
    return pl.pallas_call(
        functools.partial(_post_kernel, n_mix=len(mixes), final=final),
        grid=(n // tm,),
        in_specs=in_specs,
        out_specs=row(D_MODEL),
        out_shape=jax.ShapeDtypeStruct((n, D_MODEL), F32),
        compiler_params=pltpu.CompilerParams(dimension_semantics=("parallel",), vmem_limit_bytes=VMEM_LIMIT),
        name="post_final" if final else "post",
    )(*args)


def _odd_proj_kernel(h_ref, g_ref, w_ref, qn_ref, kn_ref, cos_ref, sin_ref, q_ref, k_ref, v_ref, *, q_scale):
    hn = _rms(h_ref[...], g_ref[...], D_MODEL).astype(BF16)
    proj = _dot(hn, w_ref[...])
    cos, sin = cos_ref[...], sin_ref[...]
    qn, kn = qn_ref[...], kn_ref[...]
    for h in range(GQA_HEADS):
        sl = slice(h * LANES, (h + 1) * LANES)
        q_ref[:, sl] = (_rope(_rms(proj[:, sl], qn, LANES), cos, sin) * q_scale).astype(BF16)
    for h in range(GQA_KV_HEADS):
        sl = slice(h * LANES, (h + 1) * LANES)
        k_ref[:, sl] = _rope(_rms(proj[:, 1024 + h * LANES:1024 + (h + 1) * LANES], kn, LANES), cos, sin).astype(BF16)
    v_ref[...] = proj[:, 1280:1536].astype(BF16)


def _odd_proj(h2, g, w, qn, kn, cos, sin, q_scale):
    n = h2.shape[0]
    tm = ROW_TILE
    ns = SEQ // tm
    row = lambda w_: pl.BlockSpec((tm, w_), lambda i: (i, 0))
    full = lambda a: pl.BlockSpec(a.shape, lambda i: (0, 0))
    tab = pl.BlockSpec((tm, LANES), lambda i: (i % ns, 0))
    out_w = (1024, 256, 256)
    return pl.pallas_call(
        functools.partial(_odd_proj_kernel, q_scale=q_scale),
        grid=(n // tm,),
        in_specs=[row(D_MODEL), full(g), full(w), full(qn), full(kn), tab, tab],
        out_specs=[row(w_) for w_ in out_w],
        out_shape=[jax.ShapeDtypeStruct((n, w_), BF16) for w_ in out_w],
        compiler_params=pltpu.CompilerParams(dimension_semantics=("parallel",), vmem_limit_bytes=VMEM_LIMIT),
        name="odd_proj",
    )(h2, g, w, qn, kn, cos, sin)


def _prep_even_weights(w_in, q_norm, w_uq, kv_norm, w_ukv):
    o_kv = MLA_Q_RANK
    o_kr = o_kv + MLA_KV_RANK
    o_d = o_kr + MLA_ROPE
    w_cq, w_ckv, w_kr = w_in[:, :o_kv], w_in[:, o_kv:o_kr], w_in[:, o_kr:o_d]
    w_qd, w_kd, w_vd = w_in[:, o_d:o_d + 512], w_in[:, o_d + 512:o_d + 1024], w_in[:, o_d + 1024:o_d + 1536]
    slab = lambda w: _cat_slices(w.reshape(D_MODEL, DIFF_HEADS, LANES), _DIFF_PIECES).reshape(D_MODEL, DIFF_HEADS * LANES)
    wbig = jnp.concatenate(
        [slab(w_qd), slab(w_kd), w_vd, w_cq, jnp.zeros((D_MODEL, 256 - MLA_Q_RANK), F32), w_ckv,
         _cat_slices(w_kr, _MLA_KROPE_PIECES)], axis=1).astype(BF16)

    qn = jnp.concatenate([q_norm, jnp.zeros((256 - MLA_Q_RANK,), F32)])[None, :]
    uq = _cat_slices(w_uq.reshape(MLA_Q_RANK, MLA_HEADS, MLA_NOPE + MLA_ROPE), _MLA_QK_PIECES)
    wuq = jnp.concatenate([uq.reshape(MLA_Q_RANK, MLA_HEADS * LANES),
                           jnp.zeros((256 - MLA_Q_RANK, MLA_HEADS * LANES), F32)], axis=0).astype(BF16)

    ukv = w_ukv.reshape(MLA_KV_RANK, MLA_HEADS, MLA_NOPE + MLA_V)
    wk = _cat_slices(ukv[:, :, :MLA_NOPE], _MLA_KNOPE_PIECES)
    wv = ukv[:, :, MLA_NOPE:]
    wukv = jnp.concatenate([wk.reshape(MLA_KV_RANK, -1), wv.reshape(MLA_KV_RANK, -1)], axis=1).astype(BF16)
    return wbig, qn, wuq, kv_norm[None, :], wukv


def _prep_odd_weights(w_qkv, q_norm, k_norm):
    nqk = GQA_HEADS + GQA_KV_HEADS
    qk = _cat_slices(w_qkv[:, :nqk * LANES].reshape(D_MODEL, nqk, LANES), _AXIAL_PIECES).reshape(D_MODEL, nqk * LANES)
    w = jnp.concatenate([qk, w_qkv[:, nqk * LANES:]], axis=1).astype(BF16)
    return w, _cat_slices(q_norm, _AXIAL_PIECES)[None, :], _cat_slices(k_norm, _AXIAL_PIECES)[None, :]


def kernel(x, e_attn_norm, e_w_in, e_q_norm, e_w_uq, e_kv_norm, e_w_ukv, e_lambda_q1, e_lambda_k1, e_lambda_q2, e_lambda_k2, e_subln, e_w_out, o_attn_norm, o_w_qkv, o_q_norm, o_k_norm, o_w_out, ffn_norm, w_gate, w_up, w_down, final_norm):
    b, s, d = x.shape
    assert (s, d) == (SEQ, D_MODEL)
    n = b * s
    log2e = math.log2(math.e)
    dcos, dsin, mcos, msin, acos, asin = _rope_tables()
    h = x.reshape(n, d)

    wbig, qn, wuq, kvn, wukv = _prep_even_weights(e_w_in[0], e_q_norm[0], e_w_uq[0], e_kv_norm[0], e_w_ukv[0])
    qm, km, vm, qd, kd, vd = _even_proj(
        h, e_attn_norm[0][None, :], wbig, qn, wuq, kvn, wukv, (dcos, dsin, mcos, msin),
        q_scale_mla=(MLA_NOPE + MLA_ROPE) ** -0.5 * log2e, q_scale_diff=DIFF_DIM ** -0.5 * log2e)
    o_mla = _attention(_mla_attn_kernel, qm, km, vm, [], q_w=2 * LANES, k_w=2 * LANES, v_w=LANES, o_w=LANES,
                       tq=ATT_ROWS, n_groups=MLA_HEADS // 2, name="mla_attn")
    lam_init = 0.8 - 0.6 * math.exp(-0.3 * 0)
    lam_args = [e_lambda_q1[0][None, :], e_lambda_k1[0][None, :], e_lambda_q2[0][None, :], e_lambda_k2[0][None, :],
                e_subln[0][None, :]]
    o_diff = _attention(functools.partial(_diff_attn_kernel, lam_init=lam_init), qd, kd, vd, lam_args,
                        q_w=LANES, k_w=LANES, v_w=LANES, o_w=LANES, tq=ATT_ROWS // 2, n_groups=DIFF_HEADS,
                        name="diff_attn")
    w_out = e_w_out[0].astype(BF16)
    h = _post(h, [o_mla, o_diff], [w_out[:512], w_out[512:]], ffn_norm[0][None, :],
              w_gate[0].astype(BF16), w_up[0].astype(BF16), w_down[0].astype(BF16))

    wqkv, oqn, okn = _prep_odd_weights(o_w_qkv[0], o_q_norm[0], o_k_norm[0])
    q, k, v = _odd_proj(h, o_attn_norm[0][None, :], wqkv, oqn, okn, acos, asin, q_scale=LANES ** -0.5 * log2e)
    o = _attention(_gqa_attn_kernel, q, k, v, [], q_w=GQA_GROUP * LANES, k_w=LANES, v_w=LANES, o_w=GQA_GROUP * LANES,
                   tq=ATT_ROWS // GQA_GROUP, n_groups=GQA_KV_HEADS, name="gqa_attn")
    out = _post(h, [o], [o_w_out[0].astype(BF16)], ffn_norm[1][None, :],
                w_gate[1].astype(BF16), w_up[1].astype(BF16), w_down[1].astype(BF16), final_gain=final_norm[None, :])
    return out.reshape(b, s, d)
```

```python
import functools
import math

import jax
import jax.numpy as jnp
from jax import lax
from jax.experimental import pallas as pl
from jax.experimental.pallas import tpu as pltpu

F32 = jnp.float32
BF16 = jnp.bfloat16

D_MODEL = 1024
SEQ = 4096
GRID_W = 64
EPS = 1e-6
ROPE_THETA = 500000.0
AXIAL_THETA = 10000.0

MLA_HEADS = 8
MLA_Q_RANK = 192
MLA_KV_RANK = 128
MLA_NOPE = 64
MLA_ROPE = 32
MLA_V = 64
DIFF_HEADS = 4
DIFF_DIM = 64
DIFF_ROT = 16
GQA_HEADS = 8
GQA_KV_HEADS = 2
GQA_GROUP = 4
FFN_HIDDEN = 2816

LANES = 128
HALF = LANES // 2
VMEM_LIMIT = 56 * 1024 * 1024

ROW_TILE = 512
ATT_ROWS = 2048
KV_TILE = 256
ODD_ROW_TILE = 1024
ROW_CHUNK = 256
NEG_INIT = -1e30


def _cat_slices(a, pieces):
    parts = []
    for p in pieces:
        if isinstance(p, int):
            parts.append(jnp.zeros(a.shape[:-1] + (p,), a.dtype))
        else:
            parts.append(a[..., p[0]:p[1]])
    return jnp.concatenate(parts, axis=-1)


_DIFF_PIECES = ((0, 8), (16, 40), (64, 72), (80, 104), (8, 16), (40, 64), (72, 80), (104, 128))
_MLA_QK_PIECES = ((0, 48), (64, 80), (48, 64), 32, (80, 96))
_MLA_KNOPE_PIECES = ((0, 48), 16, (48, 64), 48)
_MLA_KROPE_PIECES = (48, (0, 16), 48, (16, 32))
_AXIAL_PIECES = ((0, 32), (64, 96), (32, 64), (96, 128))


def _rope_tables():
    pos = jnp.arange(SEQ, dtype=jnp.int32).astype(F32)
    row = (jnp.arange(SEQ, dtype=jnp.int32) // GRID_W).astype(F32)
    col = (jnp.arange(SEQ, dtype=jnp.int32) % GRID_W).astype(F32)

    def cs(p, dim, theta):
        inv = theta ** (-jnp.arange(0, dim, 2, dtype=F32) / dim)
        ang = p[:, None] * inv[None, :]
        return jnp.cos(ang), jnp.sin(ang)

    ones = lambda n: jnp.ones((SEQ, n), F32)
    zeros = lambda n: jnp.zeros((SEQ, n), F32)

    c, s = cs(pos, DIFF_ROT, ROPE_THETA)
    cblk = jnp.concatenate([c, ones(24)], axis=1)
    sblk = jnp.concatenate([s, zeros(24)], axis=1)
    diff_cos = jnp.concatenate([cblk] * 4, axis=1)
    diff_sin = jnp.concatenate([-sblk, -sblk, sblk, sblk], axis=1)

    c, s = cs(pos, MLA_ROPE, ROPE_THETA)
    mla_cos = jnp.concatenate([ones(48), c, ones(48), c], axis=1)
    mla_sin = jnp.concatenate([zeros(48), -s, zeros(48), s], axis=1)

    cr, sr = cs(row, HALF, AXIAL_THETA)
    cc, sc = cs(col, HALF, AXIAL_THETA)
    ax_cos = jnp.concatenate([cr, cc, cr, cc], axis=1)
    ax_sin = jnp.concatenate([-sr, -sc, sr, sc], axis=1)
    return diff_cos, diff_sin, mla_cos, mla_sin, ax_cos, ax_sin


def _rms(x, gain, n):
    ms = jnp.sum(x * x, axis=-1, keepdims=True) * (1.0 / n)
    return x * lax.rsqrt(ms + EPS) * gain


def _rope(x, cos, sin):
    return x * cos + pltpu.roll(x, HALF, 1) * sin


def _dot(a, b):
    return jnp.dot(a, b, preferred_element_type=F32)


def _even_proj_kernel(x_ref, g_ref, wbig_ref, qn_ref, wuq_ref, kvn_ref, wukv_ref,
                      dcos_ref, dsin_ref, mcos_ref, msin_ref,
                      qm_ref, km_ref, vm_ref, qd_ref, kd_ref, vd_ref, *, q_scale_mla, q_scale_diff):
    x = x_ref[...]
    xn = _rms(x, g_ref[...], D_MODEL).astype(BF16)
    big = _dot(xn, wbig_ref[...])
    dcos, dsin = dcos_ref[...], dsin_ref[...]
    mcos, msin = mcos_ref[...], msin_ref[...]

    for h in range(DIFF_HEADS):
        sl = slice(h * LANES, (h + 1) * LANES)
        qd_ref[:, sl] = (_rope(big[:, sl], dcos, dsin) * q_scale_diff).astype(BF16)
        kd_ref[:, sl] = _rope(big[:, 512 + h * LANES:512 + (h + 1) * LANES], dcos, dsin).astype(BF16)
    vd_ref[...] = big[:, 1024:1536].astype(BF16)

    cq = big[:, 1536:1792]
    cqn = _rms(cq, qn_ref[...], MLA_Q_RANK).astype(BF16)
    q = _dot(cqn, wuq_ref[...])
    ckv = big[:, 1792:1920]
    ckvn = _rms(ckv, kvn_ref[...], MLA_KV_RANK).astype(BF16)
    kv = _dot(ckvn, wukv_ref[...])
    kr = _rope(big[:, 1920:2048], mcos, msin)
    for h in range(MLA_HEADS):
        sl = slice(h * LANES, (h + 1) * LANES)
        qm_ref[:, sl] = (_rope(q[:, sl], mcos, msin) * q_scale_mla).astype(BF16)
        km_ref[:, sl] = (kv[:, sl] + kr).astype(BF16)
    vm_ref[...] = kv[:, 1024:1536].astype(BF16)


def _even_proj(x2, g, wbig, qn, wuq, kvn, wukv, tabs, q_scale_mla, q_scale_diff):
    n = x2.shape[0]
    tm = ROW_TILE
    ns = SEQ // tm
    row = lambda w: pl.BlockSpec((tm, w), lambda i: (i, 0))
    full = lambda a: pl.BlockSpec(a.shape, lambda i: (0, 0))
    tab = pl.BlockSpec((tm, LANES), lambda i: (i % ns, 0))
    dcos, dsin, mcos, msin = tabs
    out_w = (1024, 1024, 512, 512, 512, 512)
    return pl.pallas_call(
        functools.partial(_even_proj_kernel, q_scale_mla=q_scale_mla, q_scale_diff=q_scale_diff),
        grid=(n // tm,),
        in_specs=[row(D_MODEL), full(g), full(wbig), full(qn), full(wuq), full(kvn), full(wukv), tab, tab, tab, tab],
        out_specs=[row(w) for w in out_w],
        out_shape=[jax.ShapeDtypeStruct((n, w), BF16) for w in out_w],
        compiler_params=pltpu.CompilerParams(dimension_semantics=("parallel",), vmem_limit_bytes=VMEM_LIMIT),
        name="even_proj",
    )(x2, g, wbig, qn, wuq, kvn, wukv, dcos, dsin, mcos, msin)


def _flash_rows(q, k_ref, v_ref, kcol, vcol):
    rows = q.shape[0]
    m = jnp.full((rows, 1), NEG_INIT, F32)
    l = jnp.zeros((rows, LANES), F32)
    acc = jnp.zeros((rows, LANES), F32)
    for start in range(0, k_ref.shape[0], KV_TILE):
        ks = k_ref[start:start + KV_TILE, kcol:kcol + LANES]
        vs = v_ref[start:start + KV_TILE, vcol:vcol + LANES]
        s = lax.dot_general(q, ks, (((1,), (1,)), ((), ())), preferred_element_type=F32)
        m_new = jnp.maximum(m, jnp.max(s, axis=-1, keepdims=True))
        alpha = jnp.exp2(m - m_new)
        p = jnp.exp2(s - m_new)
        psum = p[:, 0:LANES]
        for c in range(1, KV_TILE // LANES):
            psum = psum + p[:, c * LANES:(c + 1) * LANES]
        l = alpha * l + psum
        acc = alpha * acc + _dot(p.astype(BF16), vs)
        m = m_new
    return acc, jnp.sum(l, axis=-1, keepdims=True)


def _mla_attn_kernel(q_ref, k_ref, v_ref, o_ref):
    acc0, l0 = _flash_rows(q_ref[:, 0:LANES], k_ref, v_ref, 0, 0)
    acc1, l1 = _flash_rows(q_ref[:, LANES:2 * LANES], k_ref, v_ref, LANES, 0)
    lane = lax.broadcasted_iota(jnp.int32, acc0.shape, 1)
    o = jnp.where(lane < MLA_V, acc0 * (1.0 / l0), acc1 * (1.0 / l1))
    o_ref[...] = o.astype(BF16)


def _diff_attn_kernel(q_ref, k_ref, v_ref, lq1_ref, lk1_ref, lq2_ref, lk2_ref, sub_ref, o_ref, *, lam_init):
    q = q_ref[...]
    tq = q.shape[0]
    lane = lax.broadcasted_iota(jnp.int32, q.shape, 1)
    comp0 = (lane % HALF) < (HALF // 2)
    zero = jnp.zeros_like(q)
    q2 = jnp.concatenate([jnp.where(comp0, q, zero), jnp.where(comp0, zero, q)], axis=0)
    acc, l = _flash_rows(q2, k_ref, v_ref, 0, 0)
    o = acc * (1.0 / l)
    lam = (jnp.exp(jnp.sum(lq1_ref[...] * lk1_ref[...], axis=-1, keepdims=True))
           - jnp.exp(jnp.sum(lq2_ref[...] * lk2_ref[...], axis=-1, keepdims=True)) + lam_init)
    od = o[:tq] - lam * o[tq:]
    od = _rms(od, sub_ref[...], LANES) * (1.0 - lam_init)
    o_ref[...] = od.astype(BF16)


def _gqa_attn_kernel(q_ref, k_ref, v_ref, o_ref):
    tq = q_ref.shape[0]
    q = jnp.concatenate([q_ref[:, g * LANES:(g + 1) * LANES] for g in range(GQA_GROUP)], axis=0)
    acc, l = _flash_rows(q, k_ref, v_ref, 0, 0)
    o = acc * (1.0 / l)
    for g in range(GQA_GROUP):
        o_ref[:, g * LANES:(g + 1) * LANES] = o[g * tq:(g + 1) * tq].astype(BF16)


def _attention(body, q, k, v, extra, *, q_w, k_w, v_w, o_w, tq, n_groups, name):
    n = q.shape[0]
    b = n // SEQ
    nq = SEQ // tq
    in_specs = [
        pl.BlockSpec((tq, q_w), lambda bi, g, i: (bi * nq + i, g)),
        pl.BlockSpec((SEQ, k_w), lambda bi, g, i: (bi, g)),
        pl.BlockSpec((SEQ, v_w), lambda bi, g, i: (bi, g)),
    ] + [pl.BlockSpec(a.shape, lambda bi, g, i: (0, 0)) for a in extra]
    return pl.pallas_call(
        body,
        grid=(b, n_groups, nq),
        in_specs=in_specs,
        out_specs=pl.BlockSpec((tq, o_w), lambda bi, g, i: (bi * nq + i, g)),
        out_shape=jax.ShapeDtypeStruct((n, n_groups * o_w), BF16),
        compiler_params=pltpu.CompilerParams(
            dimension_semantics=("parallel", "parallel", "parallel"), vmem_limit_bytes=VMEM_LIMIT),
        name=name,
    )(q, k, v, *extra)


def _post_kernel(*refs, n_mix, final):
    h_ref = refs[0]
    o_refs = refs[1:1 + n_mix]
    w_refs = refs[1 + n_mix:1 + 2 * n_mix]
    g_ref, wg_ref, wu_ref, wd_ref = refs[1 + 2 * n_mix:5 + 2 * n_mix]
    rest = refs[5 + 2 * n_mix:]
    if final:
        fg_ref, out_ref = rest
    else:
        (out_ref,) = rest
    h = h_ref[...]
    for o_ref, w_ref in zip(o_refs, w_refs):
        h = h + _dot(o_ref[...], w_ref[...])
    hn = _rms(h, g_ref[...], D_MODEL).astype(BF16)
    gate = _dot(hn, wg_ref[...])
    up = _dot(hn, wu_ref[...])
    act = (gate * (1.0 / (1.0 + jnp.exp(-gate))) * up).astype(BF16)
    h = h + _dot(act, wd_ref[...])
    if final:
        h = _rms(h, fg_ref[...], D_MODEL)
    out_ref[...] = h


def _post(h2, mixes, w_outs, g, wg, wu, wd, final_gain=None):
    n = h2.shape[0]
    tm = ROW_TILE
    row = lambda w: pl.BlockSpec((tm, w), lambda i: (i, 0))
    const = lambda a: pl.BlockSpec(a.shape, lambda i: (0, 0), pipeline_mode=pl.Buffered(1))
    final = final_gain is not None
    args = [h2, *mixes, *w_outs, g, wg, wu, wd] + ([final_gain] if final else [])
    in_specs = ([row(D_MODEL)] + [row(m.shape[1]) for m in mixes] + [const(w) for w in w_outs]
                + [const(g), const(wg), const(wu), const(wd)] + ([const(final_gain)] if final else []))
    return pl.pallas_call(
        functools.partial(_post_kernel, n_mix=len(mixes), final=final),
        grid=(n // tm,),
        in_specs=in_specs,
        out_specs=row(D_MODEL),
        out_shape=jax.ShapeDtypeStruct((n, D_MODEL), F32),
        compiler_params=pltpu.CompilerParams(dimension_semantics=("parallel",), vmem_limit_bytes=VMEM_LIMIT),
        name="post_final" if final else "post",
    )(*args)


def _odd_proj_kernel(h_ref, g_ref, w_ref, qn_ref, kn_ref, cos_ref, sin_ref, q_ref, k_ref, v_ref, *, q_scale):
    qn, kn = qn_ref[...], kn_ref[...]
    for r0 in range(0, h_ref.shape[0], ROW_CHUNK):
        rows = slice(r0, r0 + ROW_CHUNK)
        hn = _rms(h_ref[rows, :], g_ref[...], D_MODEL).astype(BF16)
        proj = _dot(hn, w_ref[...])
        cos, sin = cos_ref[rows, :], sin_ref[rows, :]
        for h in range(GQA_HEADS):
            sl = slice(h * LANES, (h + 1) * LANES)
            q_ref[rows, sl] = (_rope(_rms(proj[:, sl], qn, LANES), cos, sin) * q_scale).astype(BF16)
        for h in range(GQA_KV_HEADS):
            sl = slice(h * LANES, (h + 1) * LANES)
            k_ref[rows, sl] = _rope(_rms(proj[:, 1024 + h * LANES:1024 + (h + 1) * LANES], kn, LANES), cos, sin).astype(BF16)
        v_ref[rows, :] = proj[:, 1280:1536].astype(BF16)


def _odd_proj(h2, g, w, qn, kn, cos, sin, q_scale):
    n = h2.shape[0]
    tm = ODD_ROW_TILE
    ns = SEQ // tm
    row = lambda w_: pl.BlockSpec((tm, w_), lambda i: (i, 0))
    full = lambda a: pl.BlockSpec(a.shape, lambda i: (0, 0))
    tab = pl.BlockSpec((tm, LANES), lambda i: (i % ns, 0))
    out_w = (1024, 256, 256)
    return pl.pallas_call(
        functools.partial(_odd_proj_kernel, q_scale=q_scale),
        grid=(n // tm,),
        in_specs=[row(D_MODEL), full(g), full(w), full(qn), full(kn), tab, tab],
        out_specs=[row(w_) for w_ in out_w],
        out_shape=[jax.ShapeDtypeStruct((n, w_), BF16) for w_ in out_w],
        compiler_params=pltpu.CompilerParams(dimension_semantics=("parallel",), vmem_limit_bytes=VMEM_LIMIT),
        name="odd_proj",
    )(h2, g, w, qn, kn, cos, sin)


def _prep_even_weights(w_in, q_norm, w_uq, kv_norm, w_ukv):
    o_kv = MLA_Q_RANK
    o_kr = o_kv + MLA_KV_RANK
    o_d = o_kr + MLA_ROPE
    w_cq, w_ckv, w_kr = w_in[:, :o_kv], w_in[:, o_kv:o_kr], w_in[:, o_kr:o_d]
    w_qd, w_kd, w_vd = w_in[:, o_d:o_d + 512], w_in[:, o_d + 512:o_d + 1024], w_in[:, o_d + 1024:o_d + 1536]
    slab = lambda w: _cat_slices(w.reshape(D_MODEL, DIFF_HEADS, LANES), _DIFF_PIECES).reshape(D_MODEL, DIFF_HEADS * LANES)
    wbig = jnp.concatenate(
        [slab(w_qd), slab(w_kd), w_vd, w_cq, jnp.zeros((D_MODEL, 256 - MLA_Q_RANK), F32), w_ckv,
         _cat_slices(w_kr, _MLA_KROPE_PIECES)], axis=1).astype(BF16)

    qn = jnp.concatenate([q_norm, jnp.zeros((256 - MLA_Q_RANK,), F32)])[None, :]
    uq = _cat_slices(w_uq.reshape(MLA_Q_RANK, MLA_HEADS, MLA_NOPE + MLA_ROPE), _MLA_QK_PIECES)
    wuq = jnp.concatenate([uq.reshape(MLA_Q_RANK, MLA_HEADS * LANES),
                           jnp.zeros((256 - MLA_Q_RANK, MLA_HEADS * LANES), F32)], axis=0).astype(BF16)

    ukv = w_ukv.reshape(MLA_KV_RANK, MLA_HEADS, MLA_NOPE + MLA_V)
    wk = _cat_slices(ukv[:, :, :MLA_NOPE], _MLA_KNOPE_PIECES)
    wv = ukv[:, :, MLA_NOPE:]
    wukv = jnp.concatenate([wk.reshape(MLA_KV_RANK, -1), wv.reshape(MLA_KV_RANK, -1)], axis=1).astype(BF16)
    return wbig, qn, wuq, kv_norm[None, :], wukv


def _prep_odd_weights(w_qkv, q_norm, k_norm):
    nqk = GQA_HEADS + GQA_KV_HEADS
    qk = _cat_slices(w_qkv[:, :nqk * LANES].reshape(D_MODEL, nqk, LANES), _AXIAL_PIECES).reshape(D_MODEL, nqk * LANES)
    w = jnp.concatenate([qk, w_qkv[:, nqk * LANES:]], axis=1).astype(BF16)
    return w, _cat_slices(q_norm, _AXIAL_PIECES)[None, :], _cat_slices(k_norm, _AXIAL_PIECES)[None, :]


def kernel(x, e_attn_norm, e_w_in, e_q_norm, e_w_uq, e_kv_norm, e_w_ukv, e_lambda_q1, e_lambda_k1, e_lambda_q2, e_lambda_k2, e_subln, e_w_out, o_attn_norm, o_w_qkv, o_q_norm, o_k_norm, o_w_out, ffn_norm, w_gate, w_up, w_down, final_norm):
    b, s, d = x.shape
    assert (s, d) == (SEQ, D_MODEL)
    n = b * s
    log2e = math.log2(math.e)
    dcos, dsin, mcos, msin, acos, asin = _rope_tables()
    h = x.reshape(n, d)

    wbig, qn, wuq, kvn, wukv = _prep_even_weights(e_w_in[0], e_q_norm[0], e_w_uq[0], e_kv_norm[0], e_w_ukv[0])
    qm, km, vm, qd, kd, vd = _even_proj(
        h, e_attn_norm[0][None, :], wbig, qn, wuq, kvn, wukv, (dcos, dsin, mcos, msin),
        q_scale_mla=(MLA_NOPE + MLA_ROPE) ** -0.5 * log2e, q_scale_diff=DIFF_DIM ** -0.5 * log2e)
    o_mla = _attention(_mla_attn_kernel, qm, km, vm, [], q_w=2 * LANES, k_w=2 * LANES, v_w=LANES, o_w=LANES,
                       tq=ATT_ROWS, n_groups=MLA_HEADS // 2, name="mla_attn")
    lam_init = 0.8 - 0.6 * math.exp(-0.3 * 0)
    lam_args = [e_lambda_q1[0][None, :], e_lambda_k1[0][None, :], e_lambda_q2[0][None, :], e_lambda_k2[0][None, :],
                e_subln[0][None, :]]
    o_diff = _attention(functools.partial(_diff_attn_kernel, lam_init=lam_init), qd, kd, vd, lam_args,
                        q_w=LANES, k_w=LANES, v_w=LANES, o_w=LANES, tq=ATT_ROWS // 2, n_groups=DIFF_HEADS,
                        name="diff_attn")
    w_out = e_w_out[0].astype(BF16)
    h = _post(h, [o_mla, o_diff], [w_out[:512], w_out[512:]], ffn_norm[0][None, :],
              w_gate[0].astype(BF16), w_up[0].astype(BF16), w_down[0].astype(BF16))

    wqkv, oqn, okn = _prep_odd_weights(o_w_qkv[0], o_q_norm[0], o_k_norm[0])
    q, k, v = _odd_proj(h, o_attn_norm[0][None, :], wqkv, oqn, okn, acos, asin, q_scale=LANES ** -0.5 * log2e)
    o = _attention(_gqa_attn_kernel, q, k, v, [], q_w=GQA_GROUP * LANES, k_w=LANES, v_w=LANES, o_w=GQA_GROUP * LANES,
                   tq=ATT_ROWS // GQA_GROUP, n_groups=GQA_KV_HEADS, name="gqa_attn")
    out = _post(h, [o], [o_w_out[0].astype(BF16)], ffn_norm[1][None, :],
                w_gate[1].astype(BF16), w_up[1].astype(BF16), w_down[1].astype(BF16), final_gain=final_norm[None, :])
    return out.reshape(b, s, d)
```

```python
import functools
import math

import jax
import jax.numpy as jnp
import numpy as np
from jax import lax
from jax.experimental import pallas as pl
from jax.experimental.pallas import tpu as pltpu

F32 = jnp.float32
BF16 = jnp.bfloat16

D_MODEL = 1024
SEQ = 4096
GRID_W = 64
EPS = 1e-6
ROPE_THETA = 500000.0
AXIAL_THETA = 10000.0

MLA_HEADS = 8
MLA_Q_RANK = 192
MLA_KV_RANK = 128
MLA_NOPE = 64
MLA_ROPE = 32
MLA_V = 64
DIFF_HEADS = 4
DIFF_DIM = 64
DIFF_ROT = 16
GQA_HEADS = 8
GQA_KV_HEADS = 2
GQA_GROUP = 4
FFN_HIDDEN = 2816

LANES = 128
HALF = LANES // 2
VMEM_LIMIT = 56 * 1024 * 1024

ROW_TILE = 512
ATT_ROWS = 2048
KV_TILE = 256
PROJ_ROW_TILE = 1024
ROW_CHUNK = 256
NEG_INIT = -1e30


def _cat_slices(a, pieces):
    parts = []
    for p in pieces:
        if isinstance(p, int):
            parts.append(jnp.zeros(a.shape[:-1] + (p,), a.dtype))
        else:
            parts.append(a[..., p[0]:p[1]])
    return jnp.concatenate(parts, axis=-1)


_DIFF_PIECES = ((0, 8), (16, 40), (64, 72), (80, 104), (8, 16), (40, 64), (72, 80), (104, 128))
_MLA_QK_PIECES = ((0, 48), (64, 80), (48, 64), 32, (80, 96))
_MLA_KNOPE_PIECES = ((0, 48), 16, (48, 64), 48)
_MLA_KROPE_PIECES = (48, (0, 16), 48, (16, 32))
_AXIAL_PIECES = ((0, 32), (64, 96), (32, 64), (96, 128))


def _rope_tables():
    f32 = np.float32
    pos = np.arange(SEQ, dtype=np.int32).astype(f32)
    row = (np.arange(SEQ, dtype=np.int32) // GRID_W).astype(f32)
    col = (np.arange(SEQ, dtype=np.int32) % GRID_W).astype(f32)

    def cs(p, dim, theta):
        inv = (f32(theta) ** (-np.arange(0, dim, 2, dtype=f32) / f32(dim))).astype(f32)
        ang = (p[:, None] * inv[None, :]).astype(f32)
        return np.cos(ang).astype(f32), np.sin(ang).astype(f32)

    ones = lambda n: np.ones((SEQ, n), f32)
    zeros = lambda n: np.zeros((SEQ, n), f32)

    c, s = cs(pos, DIFF_ROT, ROPE_THETA)
    cblk = np.concatenate([c, ones(24)], axis=1)
    sblk = np.concatenate([s, zeros(24)], axis=1)
    diff_cos = np.concatenate([cblk] * 4, axis=1)
    diff_sin = np.concatenate([-sblk, -sblk, sblk, sblk], axis=1)

    c, s = cs(pos, MLA_ROPE, ROPE_THETA)
    mla_cos = np.concatenate([ones(48), c, ones(48), c], axis=1)
    mla_sin = np.concatenate([zeros(48), -s, zeros(48), s], axis=1)

    cr, sr = cs(row, HALF, AXIAL_THETA)
    cc, sc = cs(col, HALF, AXIAL_THETA)
    ax_cos = np.concatenate([cr, cc, cr, cc], axis=1)
    ax_sin = np.concatenate([-sr, -sc, sr, sc], axis=1)
    return tuple(jnp.asarray(t) for t in (diff_cos, diff_sin, mla_cos, mla_sin, ax_cos, ax_sin))


def _rms(x, gain, n):
    ms = jnp.sum(x * x, axis=-1, keepdims=True) * (1.0 / n)
    return x * lax.rsqrt(ms + EPS) * gain


def _rope(x, cos, sin):
    return x * cos + pltpu.roll(x, HALF, 1) * sin


def _dot(a, b):
    return jnp.dot(a, b, preferred_element_type=F32)


def _even_proj_kernel(x_ref, g_ref, wbig_ref, qn_ref, wuq_ref, kvn_ref, wukv_ref,
                      dcos_ref, dsin_ref, mcos_ref, msin_ref,
                      qm_ref, km_ref, vm_ref, qd_ref, kd_ref, vd_ref, *, q_scale_mla, q_scale_diff):
    for r0 in range(0, x_ref.shape[0], ROW_CHUNK):
        rows = slice(r0, r0 + ROW_CHUNK)
        xn = _rms(x_ref[rows, :], g_ref[...], D_MODEL).astype(BF16)
        big = _dot(xn, wbig_ref[...])
        dcos, dsin = dcos_ref[rows, :], dsin_ref[rows, :]
        mcos, msin = mcos_ref[rows, :], msin_ref[rows, :]

        for h in range(DIFF_HEADS):
            sl = slice(h * LANES, (h + 1) * LANES)
            qd_ref[rows, sl] = (_rope(big[:, sl], dcos, dsin) * q_scale_diff).astype(BF16)
            kd_ref[rows, sl] = _rope(big[:, 512 + h * LANES:512 + (h + 1) * LANES], dcos, dsin).astype(BF16)
        vd_ref[rows, :] = big[:, 1024:1536].astype(BF16)

        cq = big[:, 1536:1792]
        cqn = _rms(cq, qn_ref[...], MLA_Q_RANK).astype(BF16)
        q = _dot(cqn, wuq_ref[...])
        ckv = big[:, 1792:1920]
        ckvn = _rms(ckv, kvn_ref[...], MLA_KV_RANK).astype(BF16)
        kv = _dot(ckvn, wukv_ref[...])
        kr = _rope(big[:, 1920:2048], mcos, msin)
        for h in range(MLA_HEADS):
            sl = slice(h * LANES, (h + 1) * LANES)
            qm_ref[rows, sl] = (_rope(q[:, sl], mcos, msin) * q_scale_mla).astype(BF16)
            km_ref[rows, sl] = (kv[:, sl] + kr).astype(BF16)
        vm_ref[rows, :] = kv[:, 1024:1536].astype(BF16)


def _even_proj(x2, g, wbig, qn, wuq, kvn, wukv, tabs, q_scale_mla, q_scale_diff):
    n = x2.shape[0]
    tm = PROJ_ROW_TILE
    ns = SEQ // tm
    row = lambda w: pl.BlockSpec((tm, w), lambda i: (i, 0))
    full = lambda a: pl.BlockSpec(a.shape, lambda i: (0, 0), pipeline_mode=pl.Buffered(1))
    tab = pl.BlockSpec((tm, LANES), lambda i: (i % ns, 0))
    dcos, dsin, mcos, msin = tabs
    out_w = (1024, 1024, 512, 512, 512, 512)
    return pl.pallas_call(
        functools.partial(_even_proj_kernel, q_scale_mla=q_scale_mla, q_scale_diff=q_scale_diff),
        grid=(n // tm,),
        in_specs=[row(D_MODEL), full(g), full(wbig), full(qn), full(wuq), full(kvn), full(wukv), tab, tab, tab, tab],
        out_specs=[row(w) for w in out_w],
        out_shape=[jax.ShapeDtypeStruct((n, w), BF16) for w in out_w],
        compiler_params=pltpu.CompilerParams(dimension_semantics=("parallel",), vmem_limit_bytes=VMEM_LIMIT),
        name="even_proj",
    )(x2, g, wbig, qn, wuq, kvn, wukv, dcos, dsin, mcos, msin)


def _flash_rows(q, k_ref, v_ref, kcol, vcol):
    rows = q.shape[0]
    m = jnp.full((rows, 1), NEG_INIT, F32)
    l = jnp.zeros((rows, LANES), F32)
    acc = jnp.zeros((rows, LANES), F32)
    for start in range(0, k_ref.shape[0], KV_TILE):
        ks = k_ref[start:start + KV_TILE, kcol:kcol + LANES]
        vs = v_ref[start:start + KV_TILE, vcol:vcol + LANES]
        s = lax.dot_general(q, ks, (((1,), (1,)), ((), ())), preferred_element_type=F32)
        m_new = jnp.maximum(m, jnp.max(s, axis=-1, keepdims=True))
        alpha = jnp.exp2(m - m_new)
        p = jnp.exp2(s - m_new)
        psum = p[:, 0:LANES]
        for c in range(1, KV_TILE // LANES):
            psum = psum + p[:, c * LANES:(c + 1) * LANES]
        l = alpha * l + psum
        acc = alpha * acc + _dot(p.astype(BF16), vs)
        m = m_new
    return acc, jnp.sum(l, axis=-1, keepdims=True)


def _mla_attn_kernel(q_ref, k_ref, v_ref, o_ref):
    acc0, l0 = _flash_rows(q_ref[:, 0:LANES], k_ref, v_ref, 0, 0)
    acc1, l1 = _flash_rows(q_ref[:, LANES:2 * LANES], k_ref, v_ref, LANES, 0)
    lane = lax.broadcasted_iota(jnp.int32, acc0.shape, 1)
    o = jnp.where(lane < MLA_V, acc0 * (1.0 / l0), acc1 * (1.0 / l1))
    o_ref[...] = o.astype(BF16)


def _diff_attn_kernel(q_ref, k_ref, v_ref, lq1_ref, lk1_ref, lq2_ref, lk2_ref, sub_ref, o_ref, *, lam_init):
    q = q_ref[...]
    tq = q.shape[0]
    lane = lax.broadcasted_iota(jnp.int32, q.shape, 1)
    comp0 = (lane % HALF) < (HALF // 2)
    zero = jnp.zeros_like(q)
    q2 = jnp.concatenate([jnp.where(comp0, q, zero), jnp.where(comp0, zero, q)], axis=0)
    acc, l = _flash_rows(q2, k_ref, v_ref, 0, 0)
    o = acc * (1.0 / l)
    lam = (jnp.exp(jnp.sum(lq1_ref[...] * lk1_ref[...], axis=-1, keepdims=True))
           - jnp.exp(jnp.sum(lq2_ref[...] * lk2_ref[...], axis=-1, keepdims=True)) + lam_init)
    od = o[:tq] - lam * o[tq:]
    od = _rms(od, sub_ref[...], LANES) * (1.0 - lam_init)
    o_ref[...] = od.astype(BF16)


def _gqa_attn_kernel(q_ref, k_ref, v_ref, o_ref):
    tq = q_ref.shape[0]
    q = jnp.concatenate([q_ref[:, g * LANES:(g + 1) * LANES] for g in range(GQA_GROUP)], axis=0)
    acc, l = _flash_rows(q, k_ref, v_ref, 0, 0)
    o = acc * (1.0 / l)
    for g in range(GQA_GROUP):
        o_ref[:, g * LANES:(g + 1) * LANES] = o[g * tq:(g + 1) * tq].astype(BF16)


def _attention(body, q, k, v, extra, *, q_w, k_w, v_w, o_w, tq, n_groups, name):
    n = q.shape[0]
    b = n // SEQ
    nq = SEQ // tq
    in_specs = [
        pl.BlockSpec((tq, q_w), lambda bi, g, i: (bi * nq + i, g)),
        pl.BlockSpec((SEQ, k_w), lambda bi, g, i: (bi, g)),
        pl.BlockSpec((SEQ, v_w), lambda bi, g, i: (bi, g)),
    ] + [pl.BlockSpec(a.shape, lambda bi, g, i: (0, 0)) for a in extra]
    return pl.pallas_call(
        body,
        grid=(b, n_groups, nq),
        in_specs=in_specs,
        out_specs=pl.BlockSpec((tq, o_w), lambda bi, g, i: (bi * nq + i, g)),
        out_shape=jax.ShapeDtypeStruct((n, n_groups * o_w), BF16),
        compiler_params=pltpu.CompilerParams(
            dimension_semantics=("parallel", "parallel", "parallel"), vmem_limit_bytes=VMEM_LIMIT),
        name=name,
    )(q, k, v, *extra)


def _post_kernel(*refs, n_mix, final):
    h_ref = refs[0]
    o_refs = refs[1:1 + n_mix]
    w_refs = refs[1 + n_mix:1 + 2 * n_mix]
    g_ref, wg_ref, wu_ref, wd_ref = refs[1 + 2 * n_mix:5 + 2 * n_mix]
    rest = refs[5 + 2 * n_mix:]
    if final:
        fg_ref, out_ref = rest
    else:
        (out_ref,) = rest
    h = h_ref[...]
    for o_ref, w_ref in zip(o_refs, w_refs):
        h = h + _dot(o_ref[...], w_ref[...])
    hn = _rms(h, g_ref[...], D_MODEL).astype(BF16)
    gate = _dot(hn, wg_ref[...])
    up = _dot(hn, wu_ref[...])
    act = (gate * (1.0 / (1.0 + jnp.exp(-gate))) * up).astype(BF16)
    h = h + _dot(act, wd_ref[...])
    if final:
        h = _rms(h, fg_ref[...], D_MODEL)
    out_ref[...] = h


def _post(h2, mixes, w_outs, g, wg, wu, wd, final_gain=None):
    n = h2.shape[0]
    tm = ROW_TILE
    row = lambda w: pl.BlockSpec((tm, w), lambda i: (i, 0))
    const = lambda a: pl.BlockSpec(a.shape, lambda i: (0, 0), pipeline_mode=pl.Buffered(1))
    final = final_gain is not None
    args = [h2, *mixes, *w_outs, g, wg, wu, wd] + ([final_gain] if final else [])
    in_specs = ([row(D_MODEL)] + [row(m.shape[1]) for m in mixes] + [const(w) for w in w_outs]
                + [const(g), const(wg), const(wu), const(wd)] + ([const(final_gain)] if final else []))
    return pl.pallas_call(
        functools.partial(_post_kernel, n_mix=len(mixes), final=final),
        grid=(n // tm,),
        in_specs=in_specs,
        out_specs=row(D_MODEL),
        out_shape=jax.ShapeDtypeStruct((n, D_MODEL), F32),
        compiler_params=pltpu.CompilerParams(dimension_semantics=("parallel",), vmem_limit_bytes=VMEM_LIMIT),
        name="post_final" if final else "post",
    )(*args)


def _odd_proj_kernel(h_ref, g_ref, w_ref, qn_ref, kn_ref, cos_ref, sin_ref, q_ref, k_ref, v_ref, *, q_scale):
    qn, kn = qn_ref[...], kn_ref[...]
    for r0 in range(0, h_ref.shape[0], ROW_CHUNK):
        rows = slice(r0, r0 + ROW_CHUNK)
        hn = _rms(h_ref[rows, :], g_ref[...], D_MODEL).astype(BF16)
        proj = _dot(hn, w_ref[...])
        cos, sin = cos_ref[rows, :], sin_ref[rows, :]
        for h in range(GQA_HEADS):
            sl = slice(h * LANES, (h + 1) * LANES)
            q_ref[rows, sl] = (_rope(_rms(proj[:, sl], qn, LANES), cos, sin) * q_scale).astype(BF16)
        for h in range(GQA_KV_HEADS):
            sl = slice(h * LANES, (h + 1) * LANES)
            k_ref[rows, sl] = _rope(_rms(proj[:, 1024 + h * LANES:1024 + (h + 1) * LANES], kn, LANES), cos, sin).astype(BF16)
        v_ref[rows, :] = proj[:, 1280:1536].astype(BF16)


def _odd_proj(h2, g, w, qn, kn, cos, sin, q_scale):
    n = h2.shape[0]
    tm = PROJ_ROW_TILE
    ns = SEQ // tm
    row = lambda w_: pl.BlockSpec((tm, w_), lambda i: (i, 0))
    full = lambda a: pl.BlockSpec(a.shape, lambda i: (0, 0), pipeline_mode=pl.Buffered(1))
    tab = pl.BlockSpec((tm, LANES), lambda i: (i % ns, 0))
    out_w = (1024, 256, 256)
    return pl.pallas_call(
        functools.partial(_odd_proj_kernel, q_scale=q_scale),
        grid=(n // tm,),
        in_specs=[row(D_MODEL), full(g), full(w), full(qn), full(kn), tab, tab],
        out_specs=[row(w_) for w_ in out_w],
        out_shape=[jax.ShapeDtypeStruct((n, w_), BF16) for w_ in out_w],
        compiler_params=pltpu.CompilerParams(dimension_semantics=("parallel",), vmem_limit_bytes=VMEM_LIMIT),
        name="odd_proj",
    )(h2, g, w, qn, kn, cos, sin)


def _prep_even_weights(w_in, q_norm, w_uq, kv_norm, w_ukv):
    o_kv = MLA_Q_RANK
    o_kr = o_kv + MLA_KV_RANK
    o_d = o_kr + MLA_ROPE
    w_cq, w_ckv, w_kr = w_in[:, :o_kv], w_in[:, o_kv:o_kr], w_in[:, o_kr:o_d]
    w_qd, w_kd, w_vd = w_in[:, o_d:o_d + 512], w_in[:, o_d + 512:o_d + 1024], w_in[:, o_d + 1024:o_d + 1536]
    slab = lambda w: _cat_slices(w.reshape(D_MODEL, DIFF_HEADS, LANES), _DIFF_PIECES).reshape(D_MODEL, DIFF_HEADS * LANES)
    wbig = jnp.concatenate(
        [slab(w_qd), slab(w_kd), w_vd, w_cq, jnp.zeros((D_MODEL, 256 - MLA_Q_RANK), F32), w_ckv,
         _cat_slices(w_kr, _MLA_KROPE_PIECES)], axis=1).astype(BF16)

    qn = jnp.concatenate([q_norm, jnp.zeros((256 - MLA_Q_RANK,), F32)])[None, :]
    uq = _cat_slices(w_uq.reshape(MLA_Q_RANK, MLA_HEADS, MLA_NOPE + MLA_ROPE), _MLA_QK_PIECES)
    wuq = jnp.concatenate([uq.reshape(MLA_Q_RANK, MLA_HEADS * LANES),
                           jnp.zeros((256 - MLA_Q_RANK, MLA_HEADS * LANES), F32)], axis=0).astype(BF16)

    ukv = w_ukv.reshape(MLA_KV_RANK, MLA_HEADS, MLA_NOPE + MLA_V)
    wk = _cat_slices(ukv[:, :, :MLA_NOPE], _MLA_KNOPE_PIECES)
    wv = ukv[:, :, MLA_NOPE:]
    wukv = jnp.concatenate([wk.reshape(MLA_KV_RANK, -1), wv.reshape(MLA_KV_RANK, -1)], axis=1).astype(BF16)
    return wbig, qn, wuq, kv_norm[None, :], wukv


def _prep_odd_weights(w_qkv, q_norm, k_norm):
    nqk = GQA_HEADS + GQA_KV_HEADS
    qk = _cat_slices(w_qkv[:, :nqk * LANES].reshape(D_MODEL, nqk, LANES), _AXIAL_PIECES).reshape(D_MODEL, nqk * LANES)
    w = jnp.concatenate([qk, w_qkv[:, nqk * LANES:]], axis=1).astype(BF16)
    return w, _cat_slices(q_norm, _AXIAL_PIECES)[None, :], _cat_slices(k_norm, _AXIAL_PIECES)[None, :]


def kernel(x, e_attn_norm, e_w_in, e_q_norm, e_w_uq, e_kv_norm, e_w_ukv, e_lambda_q1, e_lambda_k1, e_lambda_q2, e_lambda_k2, e_subln, e_w_out, o_attn_norm, o_w_qkv, o_q_norm, o_k_norm, o_w_out, ffn_norm, w_gate, w_up, w_down, final_norm):
    b, s, d = x.shape
    assert (s, d) == (SEQ, D_MODEL)
    n = b * s
    log2e = math.log2(math.e)
    dcos, dsin, mcos, msin, acos, asin = _rope_tables()
    h = x.reshape(n, d)

    wbig, qn, wuq, kvn, wukv = _prep_even_weights(e_w_in[0], e_q_norm[0], e_w_uq[0], e_kv_norm[0], e_w_ukv[0])
    qm, km, vm, qd, kd, vd = _even_proj(
        h, e_attn_norm[0][None, :], wbig, qn, wuq, kvn, wukv, (dcos, dsin, mcos, msin),
        q_scale_mla=(MLA_NOPE + MLA_ROPE) ** -0.5 * log2e, q_scale_diff=DIFF_DIM ** -0.5 * log2e)
    o_mla = _attention(_mla_attn_kernel, qm, km, vm, [], q_w=2 * LANES, k_w=2 * LANES, v_w=LANES, o_w=LANES,
                       tq=ATT_ROWS, n_groups=MLA_HEADS // 2, name="mla_attn")
    lam_init = 0.8 - 0.6 * math.exp(-0.3 * 0)
    lam_args = [e_lambda_q1[0][None, :], e_lambda_k1[0][None, :], e_lambda_q2[0][None, :], e_lambda_k2[0][None, :],
                e_subln[0][None, :]]
    o_diff = _attention(functools.partial(_diff_attn_kernel, lam_init=lam_init), qd, kd, vd, lam_args,
                        q_w=LANES, k_w=LANES, v_w=LANES, o_w=LANES, tq=ATT_ROWS // 2, n_groups=DIFF_HEADS,
                        name="diff_attn")
    w_out = e_w_out[0].astype(BF16)
    h = _post(h, [o_mla, o_diff], [w_out[:512], w_out[512:]], ffn_norm[0][None, :],
              w_gate[0].astype(BF16), w_up[0].astype(BF16), w_down[0].astype(BF16))

    wqkv, oqn, okn = _prep_odd_weights(o_w_qkv[0], o_q_norm[0], o_k_norm[0])
    q, k, v = _odd_proj(h, o_attn_norm[0][None, :], wqkv, oqn, okn, acos, asin, q_scale=LANES ** -0.5 * log2e)
    o = _attention(_gqa_attn_kernel, q, k, v, [], q_w=GQA_GROUP * LANES, k_w=LANES, v_w=LANES, o_w=GQA_GROUP * LANES,
                   tq=ATT_ROWS // GQA_GROUP, n_groups=GQA_KV_HEADS, name="gqa_attn")
    out = _post(h, [o], [o_w_out[0].astype(BF16)], ffn_norm[1][None, :],
                w_gate[1].astype(BF16), w_up[1].astype(BF16), w_down[1].astype(BF16), final_gain=final_norm[None, :])
    return out.reshape(b, s, d)
```

```python
import functools
import math

import jax
import jax.numpy as jnp
import numpy as np
from jax import lax
from jax.experimental import pallas as pl
from jax.experimental.pallas import tpu as pltpu

F32 = jnp.float32
BF16 = jnp.bfloat16

D_MODEL = 1024
SEQ = 4096
GRID_W = 64
EPS = 1e-6
ROPE_THETA = 500000.0
AXIAL_THETA = 10000.0

MLA_HEADS = 8
MLA_Q_RANK = 192
MLA_KV_RANK = 128
MLA_NOPE = 64
MLA_ROPE = 32
MLA_V = 64
DIFF_HEADS = 4
DIFF_DIM = 64
DIFF_ROT = 16
GQA_HEADS = 8
GQA_KV_HEADS = 2
GQA_GROUP = 4
FFN_HIDDEN = 2816

LANES = 128
HALF = LANES // 2
VMEM_LIMIT = 56 * 1024 * 1024

ROW_TILE = 512
ATT_ROWS = 2048
KV_TILE = 256
ODD_ROW_TILE = 2048
ROW_CHUNK = 256
NEG_INIT = -1e30


def _cat_slices(a, pieces):
    parts = []
    for p in pieces:
        if isinstance(p, int):
            parts.append(jnp.zeros(a.shape[:-1] + (p,), a.dtype))
        else:
            parts.append(a[..., p[0]:p[1]])
    return jnp.concatenate(parts, axis=-1)


_DIFF_PIECES = ((0, 8), (16, 40), (64, 72), (80, 104), (8, 16), (40, 64), (72, 80), (104, 128))
_MLA_QK_PIECES = ((0, 48), (64, 80), (48, 64), 32, (80, 96))
_MLA_KNOPE_PIECES = ((0, 48), 16, (48, 64), 48)
_MLA_KROPE_PIECES = (48, (0, 16), 48, (16, 32))
_AXIAL_PIECES = ((0, 32), (64, 96), (32, 64), (96, 128))


def _rope_tables():
    f32 = np.float32
    pos = np.arange(SEQ, dtype=np.int32).astype(f32)
    row = (np.arange(SEQ, dtype=np.int32) // GRID_W).astype(f32)
    col = (np.arange(SEQ, dtype=np.int32) % GRID_W).astype(f32)

    def cs(p, dim, theta):
        inv = (f32(theta) ** (-np.arange(0, dim, 2, dtype=f32) / f32(dim))).astype(f32)
        ang = (p[:, None] * inv[None, :]).astype(f32)
        return np.cos(ang).astype(f32), np.sin(ang).astype(f32)

    ones = lambda n: np.ones((SEQ, n), f32)
    zeros = lambda n: np.zeros((SEQ, n), f32)

    c, s = cs(pos, DIFF_ROT, ROPE_THETA)
    cblk = np.concatenate([c, ones(24)], axis=1)
    sblk = np.concatenate([s, zeros(24)], axis=1)
    diff_cos = np.concatenate([cblk] * 4, axis=1)
    diff_sin = np.concatenate([-sblk, -sblk, sblk, sblk], axis=1)

    c, s = cs(pos, MLA_ROPE, ROPE_THETA)
    mla_cos = np.concatenate([ones(48), c, ones(48), c], axis=1)
    mla_sin = np.concatenate([zeros(48), -s, zeros(48), s], axis=1)

    cr, sr = cs(row, HALF, AXIAL_THETA)
    cc, sc = cs(col, HALF, AXIAL_THETA)
    ax_cos = np.concatenate([cr, cc, cr, cc], axis=1)
    ax_sin = np.concatenate([-sr, -sc, sr, sc], axis=1)
    return tuple(jnp.asarray(t) for t in (diff_cos, diff_sin, mla_cos, mla_sin, ax_cos, ax_sin))


def _rms(x, gain, n):
    ms = jnp.sum(x * x, axis=-1, keepdims=True) * (1.0 / n)
    return x * lax.rsqrt(ms + EPS) * gain


def _rope(x, cos, sin):
    return x * cos + pltpu.roll(x, HALF, 1) * sin


def _dot(a, b):
    return jnp.dot(a, b, preferred_element_type=F32)


def _even_proj_kernel(x_ref, g_ref, wbig_ref, qn_ref, wuq_ref, kvn_ref, wukv_ref,
                      dcos_ref, dsin_ref, mcos_ref, msin_ref,
                      qm_ref, km_ref, vm_ref, qd_ref, kd_ref, vd_ref, *, q_scale_mla, q_scale_diff):
    xn = _rms(x_ref[...], g_ref[...], D_MODEL).astype(BF16)
    big = _dot(xn, wbig_ref[...])
    dcos, dsin = dcos_ref[...], dsin_ref[...]
    mcos, msin = mcos_ref[...], msin_ref[...]

    for h in range(DIFF_HEADS):
        sl = slice(h * LANES, (h + 1) * LANES)
        qd_ref[:, sl] = (_rope(big[:, sl], dcos, dsin) * q_scale_diff).astype(BF16)
        kd_ref[:, sl] = _rope(big[:, 512 + h * LANES:512 + (h + 1) * LANES], dcos, dsin).astype(BF16)
    vd_ref[...] = big[:, 1024:1536].astype(BF16)

    cq = big[:, 1536:1792]
    cqn = _rms(cq, qn_ref[...], MLA_Q_RANK).astype(BF16)
    q = _dot(cqn, wuq_ref[...])
    ckv = big[:, 1792:1920]
    ckvn = _rms(ckv, kvn_ref[...], MLA_KV_RANK).astype(BF16)
    kv = _dot(ckvn, wukv_ref[...])
    kr = _rope(big[:, 1920:2048], mcos, msin)
    for h in range(MLA_HEADS):
        sl = slice(h * LANES, (h + 1) * LANES)
        qm_ref[:, sl] = (_rope(q[:, sl], mcos, msin) * q_scale_mla).astype(BF16)
        km_ref[:, sl] = (kv[:, sl] + kr).astype(BF16)
    vm_ref[...] = kv[:, 1024:1536].astype(BF16)


def _even_proj(x2, g, wbig, qn, wuq, kvn, wukv, tabs, q_scale_mla, q_scale_diff):
    n = x2.shape[0]
    tm = ROW_TILE
    ns = SEQ // tm
    row = lambda w: pl.BlockSpec((tm, w), lambda i: (i, 0))
    full = lambda a: pl.BlockSpec(a.shape, lambda i: (0, 0), pipeline_mode=pl.Buffered(1))
    tab = pl.BlockSpec((tm, LANES), lambda i: (i % ns, 0))
    dcos, dsin, mcos, msin = tabs
    out_w = (1024, 1024, 512, 512, 512, 512)
    return pl.pallas_call(
        functools.partial(_even_proj_kernel, q_scale_mla=q_scale_mla, q_scale_diff=q_scale_diff),
        grid=(n // tm,),
        in_specs=[row(D_MODEL), full(g), full(wbig), full(qn), full(wuq), full(kvn), full(wukv), tab, tab, tab, tab],
        out_specs=[row(w) for w in out_w],
        out_shape=[jax.ShapeDtypeStruct((n, w), BF16) for w in out_w],
        compiler_params=pltpu.CompilerParams(dimension_semantics=("parallel",), vmem_limit_bytes=VMEM_LIMIT),
        name="even_proj",
    )(x2, g, wbig, qn, wuq, kvn, wukv, dcos, dsin, mcos, msin)


def _flash_rows(q, k_ref, v_ref, kcol, vcol):
    rows = q.shape[0]
    m = jnp.full((rows, 1), NEG_INIT, F32)
    l = jnp.zeros((rows, LANES), F32)
    acc = jnp.zeros((rows, LANES), F32)
    for start in range(0, k_ref.shape[0], KV_TILE):
        ks = k_ref[start:start + KV_TILE, kcol:kcol + LANES]
        vs = v_ref[start:start + KV_TILE, vcol:vcol + LANES]
        s = lax.dot_general(q, ks, (((1,), (1,)), ((), ())), preferred_element_type=F32)
        m_new = jnp.maximum(m, jnp.max(s, axis=-1, keepdims=True))
        alpha = jnp.exp2(m - m_new)
        p = jnp.exp2(s - m_new)
        psum = p[:, 0:LANES]
        for c in range(1, KV_TILE // LANES):
            psum = psum + p[:, c * LANES:(c + 1) * LANES]
        l = alpha * l + psum
        acc = alpha * acc + _dot(p.astype(BF16), vs)
        m = m_new
    return acc, jnp.sum(l, axis=-1, keepdims=True)


def _mla_attn_kernel(q_ref, k_ref, v_ref, o_ref):
    acc0, l0 = _flash_rows(q_ref[:, 0:LANES], k_ref, v_ref, 0, 0)
    acc1, l1 = _flash_rows(q_ref[:, LANES:2 * LANES], k_ref, v_ref, LANES, 0)
    lane = lax.broadcasted_iota(jnp.int32, acc0.shape, 1)
    o = jnp.where(lane < MLA_V, acc0 * (1.0 / l0), acc1 * (1.0 / l1))
    o_ref[...] = o.astype(BF16)


def _diff_attn_kernel(q_ref, k_ref, v_ref, lq1_ref, lk1_ref, lq2_ref, lk2_ref, sub_ref, o_ref, *, lam_init):
    q = q_ref[...]
    tq = q.shape[0]
    lane = lax.broadcasted_iota(jnp.int32, q.shape, 1)
    comp0 = (lane % HALF) < (HALF // 2)
    zero = jnp.zeros_like(q)
    q2 = jnp.concatenate([jnp.where(comp0, q, zero), jnp.where(comp0, zero, q)], axis=0)
    acc, l = _flash_rows(q2, k_ref, v_ref, 0, 0)
    o = acc * (1.0 / l)
    lam = (jnp.exp(jnp.sum(lq1_ref[...] * lk1_ref[...], axis=-1, keepdims=True))
           - jnp.exp(jnp.sum(lq2_ref[...] * lk2_ref[...], axis=-1, keepdims=True)) + lam_init)
    od = o[:tq] - lam * o[tq:]
    od = _rms(od, sub_ref[...], LANES) * (1.0 - lam_init)
    o_ref[...] = od.astype(BF16)


def _gqa_attn_kernel(q_ref, k_ref, v_ref, o_ref):
    tq = q_ref.shape[0]
    q = jnp.concatenate([q_ref[:, g * LANES:(g + 1) * LANES] for g in range(GQA_GROUP)], axis=0)
    acc, l = _flash_rows(q, k_ref, v_ref, 0, 0)
    o = acc * (1.0 / l)
    for g in range(GQA_GROUP):
        o_ref[:, g * LANES:(g + 1) * LANES] = o[g * tq:(g + 1) * tq].astype(BF16)


def _attention(body, q, k, v, extra, *, q_w, k_w, v_w, o_w, tq, n_groups, name):
    n = q.shape[0]
    b = n // SEQ
    nq = SEQ // tq
    in_specs = [
        pl.BlockSpec((tq, q_w), lambda bi, g, i: (bi * nq + i, g)),
        pl.BlockSpec((SEQ, k_w), lambda bi, g, i: (bi, g)),
        pl.BlockSpec((SEQ, v_w), lambda bi, g, i: (bi, g)),
    ] + [pl.BlockSpec(a.shape, lambda bi, g, i: (0, 0)) for a in extra]
    return pl.pallas_call(
        body,
        grid=(b, n_groups, nq),
        in_specs=in_specs,
        out_specs=pl.BlockSpec((tq, o_w), lambda bi, g, i: (bi * nq + i, g)),
        out_shape=jax.ShapeDtypeStruct((n, n_groups * o_w), BF16),
        compiler_params=pltpu.CompilerParams(
            dimension_semantics=("parallel", "parallel", "parallel"), vmem_limit_bytes=VMEM_LIMIT),
        name=name,
    )(q, k, v, *extra)


def _post_kernel(*refs, n_mix, final):
    h_ref = refs[0]
    o_refs = refs[1:1 + n_mix]
    w_refs = refs[1 + n_mix:1 + 2 * n_mix]
    g_ref, wg_ref, wu_ref, wd_ref = refs[1 + 2 * n_mix:5 + 2 * n_mix]
    rest = refs[5 + 2 * n_mix:]
    if final:
        fg_ref, out_ref = rest
    else:
        (out_ref,) = rest
    h = h_ref[...]
    for o_ref, w_ref in zip(o_refs, w_refs):
        h = h + _dot(o_ref[...], w_ref[...])
    hn = _rms(h, g_ref[...], D_MODEL).astype(BF16)
    gate = _dot(hn, wg_ref[...])
    up = _dot(hn, wu_ref[...])
    act = (gate * (1.0 / (1.0 + jnp.exp(-gate))) * up).astype(BF16)
    h = h + _dot(act, wd_ref[...])
    if final:
        h = _rms(h, fg_ref[...], D_MODEL)
    out_ref[...] = h


def _post(h2, mixes, w_outs, g, wg, wu, wd, final_gain=None):
    n = h2.shape[0]
    tm = ROW_TILE
    row = lambda w: pl.BlockSpec((tm, w), lambda i: (i, 0))
    const = lambda a: pl.BlockSpec(a.shape, lambda i: (0, 0), pipeline_mode=pl.Buffered(1))
    final = final_gain is not None
    args = [h2, *mixes, *w_outs, g, wg, wu, wd] + ([final_gain] if final else [])
    in_specs = ([row(D_MODEL)] + [row(m.shape[1]) for m in mixes] + [const(w) for w in w_outs]
                + [const(g), const(wg), const(wu), const(wd)] + ([const(final_gain)] if final else []))
    return pl.pallas_call(
        functools.partial(_post_kernel, n_mix=len(mixes), final=final),
        grid=(n // tm,),
        in_specs=in_specs,
        out_specs=row(D_MODEL),
        out_shape=jax.ShapeDtypeStruct((n, D_MODEL), F32),
        compiler_params=pltpu.CompilerParams(dimension_semantics=("parallel",), vmem_limit_bytes=VMEM_LIMIT),
        name="post_final" if final else "post",
    )(*args)


def _odd_proj_kernel(h_ref, g_ref, w_ref, qn_ref, kn_ref, cos_ref, sin_ref, q_ref, k_ref, v_ref, *, q_scale):
    qn, kn = qn_ref[...], kn_ref[...]
    for r0 in range(0, h_ref.shape[0], ROW_CHUNK):
        rows = slice(r0, r0 + ROW_CHUNK)
        hn = _rms(h_ref[rows, :], g_ref[...], D_MODEL).astype(BF16)
        proj = _dot(hn, w_ref[...])
        cos, sin = cos_ref[rows, :], sin_ref[rows, :]
        for h in range(GQA_HEADS):
            sl = slice(h * LANES, (h + 1) * LANES)
            q_ref[rows, sl] = (_rope(_rms(proj[:, sl], qn, LANES), cos, sin) * q_scale).astype(BF16)
        for h in range(GQA_KV_HEADS):
            sl = slice(h * LANES, (h + 1) * LANES)
            k_ref[rows, sl] = _rope(_rms(proj[:, 1024 + h * LANES:1024 + (h + 1) * LANES], kn, LANES), cos, sin).astype(BF16)
        v_ref[rows, :] = proj[:, 1280:1536].astype(BF16)


def _odd_proj(h2, g, w, qn, kn, cos, sin, q_scale):
    n = h2.shape[0]
    tm = ODD_ROW_TILE
    ns = SEQ // tm
    row = lambda w_: pl.BlockSpec((tm, w_), lambda i: (i, 0))
    full = lambda a: pl.BlockSpec(a.shape, lambda i: (0, 0), pipeline_mode=pl.Buffered(1))
    tab = pl.BlockSpec((tm, LANES), lambda i: (i % ns, 0))
    out_w = (1024, 256, 256)
    return pl.pallas_call(
        functools.partial(_odd_proj_kernel, q_scale=q_scale),
        grid=(n // tm,),
        in_specs=[row(D_MODEL), full(g), full(w), full(qn), full(kn), tab, tab],
        out_specs=[row(w_) for w_ in out_w],
        out_shape=[jax.ShapeDtypeStruct((n, w_), BF16) for w_ in out_w],
        compiler_params=pltpu.CompilerParams(dimension_semantics=("parallel",), vmem_limit_bytes=VMEM_LIMIT),
        name="odd_proj",
    )(h2, g, w, qn, kn, cos, sin)


def _prep_even_weights(w_in, q_norm, w_uq, kv_norm, w_ukv):
    o_kv = MLA_Q_RANK
    o_kr = o_kv + MLA_KV_RANK
    o_d = o_kr + MLA_ROPE
    w_cq, w_ckv, w_kr = w_in[:, :o_kv], w_in[:, o_kv:o_kr], w_in[:, o_kr:o_d]
    w_qd, w_kd, w_vd = w_in[:, o_d:o_d + 512], w_in[:, o_d + 512:o_d + 1024], w_in[:, o_d + 1024:o_d + 1536]
    slab = lambda w: _cat_slices(w.reshape(D_MODEL, DIFF_HEADS, LANES), _DIFF_PIECES).reshape(D_MODEL, DIFF_HEADS * LANES)
    wbig = jnp.concatenate(
        [slab(w_qd), slab(w_kd), w_vd, w_cq, jnp.zeros((D_MODEL, 256 - MLA_Q_RANK), F32), w_ckv,
         _cat_slices(w_kr, _MLA_KROPE_PIECES)], axis=1).astype(BF16)

    qn = jnp.concatenate([q_norm, jnp.zeros((256 - MLA_Q_RANK,), F32)])[None, :]
    uq = _cat_slices(w_uq.reshape(MLA_Q_RANK, MLA_HEADS, MLA_NOPE + MLA_ROPE), _MLA_QK_PIECES)
    wuq = jnp.concatenate([uq.reshape(MLA_Q_RANK, MLA_HEADS * LANES),
                           jnp.zeros((256 - MLA_Q_RANK, MLA_HEADS * LANES), F32)], axis=0).astype(BF16)

    ukv = w_ukv.reshape(MLA_KV_RANK, MLA_HEADS, MLA_NOPE + MLA_V)
    wk = _cat_slices(ukv[:, :, :MLA_NOPE], _MLA_KNOPE_PIECES)
    wv = ukv[:, :, MLA_NOPE:]
    wukv = jnp.concatenate([wk.reshape(MLA_KV_RANK, -1), wv.reshape(MLA_KV_RANK, -1)], axis=1).astype(BF16)
    return wbig, qn, wuq, kv_norm[None, :], wukv


def _prep_odd_weights(w_qkv, q_norm, k_norm):
    nqk = GQA_HEADS + GQA_KV_HEADS
    qk = _cat_slices(w_qkv[:, :nqk * LANES].reshape(D_MODEL, nqk, LANES), _AXIAL_PIECES).reshape(D_MODEL, nqk * LANES)
    w = jnp.concatenate([qk, w_qkv[:, nqk * LANES:]], axis=1).astype(BF16)
    return w, _cat_slices(q_norm, _AXIAL_PIECES)[None, :], _cat_slices(k_norm, _AXIAL_PIECES)[None, :]


def kernel(x, e_attn_norm, e_w_in, e_q_norm, e_w_uq, e_kv_norm, e_w_ukv, e_lambda_q1, e_lambda_k1, e_lambda_q2, e_lambda_k2, e_subln, e_w_out, o_attn_norm, o_w_qkv, o_q_norm, o_k_norm, o_w_out, ffn_norm, w_gate, w_up, w_down, final_norm):
    b, s, d = x.shape
    assert (s, d) == (SEQ, D_MODEL)
    n = b * s
    log2e = math.log2(math.e)
    dcos, dsin, mcos, msin, acos, asin = _rope_tables()
    h = x.reshape(n, d)

    wbig, qn, wuq, kvn, wukv = _prep_even_weights(e_w_in[0], e_q_norm[0], e_w_uq[0], e_kv_norm[0], e_w_ukv[0])
    qm, km, vm, qd, kd, vd = _even_proj(
        h, e_attn_norm[0][None, :], wbig, qn, wuq, kvn, wukv, (dcos, dsin, mcos, msin),
        q_scale_mla=(MLA_NOPE + MLA_ROPE) ** -0.5 * log2e, q_scale_diff=DIFF_DIM ** -0.5 * log2e)
    o_mla = _attention(_mla_attn_kernel, qm, km, vm, [], q_w=2 * LANES, k_w=2 * LANES, v_w=LANES, o_w=LANES,
                       tq=ATT_ROWS, n_groups=MLA_HEADS // 2, name="mla_attn")
    lam_init = 0.8 - 0.6 * math.exp(-0.3 * 0)
    lam_args = [e_lambda_q1[0][None, :], e_lambda_k1[0][None, :], e_lambda_q2[0][None, :], e_lambda_k2[0][None, :],
                e_subln[0][None, :]]
    o_diff = _attention(functools.partial(_diff_attn_kernel, lam_init=lam_init), qd, kd, vd, lam_args,
                        q_w=LANES, k_w=LANES, v_w=LANES, o_w=LANES, tq=ATT_ROWS // 2, n_groups=DIFF_HEADS,
                        name="diff_attn")
    w_out = e_w_out[0].astype(BF16)
    h = _post(h, [o_mla, o_diff], [w_out[:512], w_out[512:]], ffn_norm[0][None, :],
              w_gate[0].astype(BF16), w_up[0].astype(BF16), w_down[0].astype(BF16))

    wqkv, oqn, okn = _prep_odd_weights(o_w_qkv[0], o_q_norm[0], o_k_norm[0])
    q, k, v = _odd_proj(h, o_attn_norm[0][None, :], wqkv, oqn, okn, acos, asin, q_scale=LANES ** -0.5 * log2e)
    o = _attention(_gqa_attn_kernel, q, k, v, [], q_w=GQA_GROUP * LANES, k_w=LANES, v_w=LANES, o_w=GQA_GROUP * LANES,
                   tq=ATT_ROWS // GQA_GROUP, n_groups=GQA_KV_HEADS, name="gqa_attn")
    out = _post(h, [o], [o_w_out[0].astype(BF16)], ffn_norm[1][None, :],
                w_gate[1].astype(BF16), w_up[1].astype(BF16), w_down[1].astype(BF16), final_gain=final_norm[None, :])
    return out.reshape(b, s, d)
```

```python
import functools
import math

import jax
import jax.numpy as jnp
import numpy as np
from jax import lax
from jax.experimental import pallas as pl
from jax.experimental.pallas import tpu as pltpu

F32 = jnp.float32
BF16 = jnp.bfloat16

D_MODEL = 1024
SEQ = 4096
GRID_W = 64
EPS = 1e-6
ROPE_THETA = 500000.0
AXIAL_THETA = 10000.0

MLA_HEADS = 8
MLA_Q_RANK = 192
MLA_KV_RANK = 128
MLA_NOPE = 64
MLA_ROPE = 32
MLA_V = 64
DIFF_HEADS = 4
DIFF_DIM = 64
DIFF_ROT = 16
GQA_HEADS = 8
GQA_KV_HEADS = 2
GQA_GROUP = 4
FFN_HIDDEN = 2816

LANES = 128
HALF = LANES // 2
VMEM_LIMIT = 56 * 1024 * 1024

ROW_TILE = 512
ATT_ROWS = 2048
KV_TILE = 256
ODD_ROW_TILE = 1024
ROW_CHUNK = 256
NEG_INIT = -1e30


def _cat_slices(a, pieces):
    parts = []
    for p in pieces:
        if isinstance(p, int):
            parts.append(jnp.zeros(a.shape[:-1] + (p,), a.dtype))
        else:
            parts.append(a[..., p[0]:p[1]])
    return jnp.concatenate(parts, axis=-1)


_DIFF_PIECES = ((0, 8), (16, 40), (64, 72), (80, 104), (8, 16), (40, 64), (72, 80), (104, 128))
_MLA_QK_PIECES = ((0, 48), (64, 80), (48, 64), 32, (80, 96))
_MLA_KNOPE_PIECES = ((0, 48), 16, (48, 64), 48)
_MLA_KROPE_PIECES = (48, (0, 16), 48, (16, 32))
_AXIAL_PIECES = ((0, 32), (64, 96), (32, 64), (96, 128))


def _rope_tables():
    f32 = np.float32
    pos = np.arange(SEQ, dtype=np.int32).astype(f32)
    row = (np.arange(SEQ, dtype=np.int32) // GRID_W).astype(f32)
    col = (np.arange(SEQ, dtype=np.int32) % GRID_W).astype(f32)

    def cs(p, dim, theta):
        inv = (f32(theta) ** (-np.arange(0, dim, 2, dtype=f32) / f32(dim))).astype(f32)
        ang = (p[:, None] * inv[None, :]).astype(f32)
        return np.cos(ang).astype(f32), np.sin(ang).astype(f32)

    ones = lambda n: np.ones((SEQ, n), f32)
    zeros = lambda n: np.zeros((SEQ, n), f32)

    c, s = cs(pos, DIFF_ROT, ROPE_THETA)
    cblk = np.concatenate([c, ones(24)], axis=1)
    sblk = np.concatenate([s, zeros(24)], axis=1)
    diff_cos = np.concatenate([cblk] * 4, axis=1)
    diff_sin = np.concatenate([-sblk, -sblk, sblk, sblk], axis=1)

    c, s = cs(pos, MLA_ROPE, ROPE_THETA)
    mla_cos = np.concatenate([ones(48), c, ones(48), c], axis=1)
    mla_sin = np.concatenate([zeros(48), -s, zeros(48), s], axis=1)

    cr, sr = cs(row, HALF, AXIAL_THETA)
    cc, sc = cs(col, HALF, AXIAL_THETA)
    ax_cos = np.concatenate([cr, cc, cr, cc], axis=1)
    ax_sin = np.concatenate([-sr, -sc, sr, sc], axis=1)
    return tuple(jnp.asarray(t) for t in (diff_cos, diff_sin, mla_cos, mla_sin, ax_cos, ax_sin))


def _rms(x, gain, n):
    ms = jnp.sum(x * x, axis=-1, keepdims=True) * (1.0 / n)
    return x * lax.rsqrt(ms + EPS) * gain


def _rope(x, cos, sin):
    return x * cos + pltpu.roll(x, HALF, 1) * sin


def _dot(a, b):
    return jnp.dot(a, b, preferred_element_type=F32)


def _even_proj_kernel(x_ref, g_ref, wbig_ref, qn_ref, wuq_ref, kvn_ref, wukv_ref,
                      dcos_ref, dsin_ref, mcos_ref, msin_ref,
                      qm_ref, km_ref, vm_ref, qd_ref, kd_ref, vd_ref, *, q_scale_mla, q_scale_diff):
    xn = _rms(x_ref[...], g_ref[...], D_MODEL).astype(BF16)
    big = _dot(xn, wbig_ref[...])
    dcos, dsin = dcos_ref[...], dsin_ref[...]
    mcos, msin = mcos_ref[...], msin_ref[...]

    for h in range(DIFF_HEADS):
        sl = slice(h * LANES, (h + 1) * LANES)
        qd_ref[:, sl] = (_rope(big[:, sl], dcos, dsin) * q_scale_diff).astype(BF16)
        kd_ref[:, sl] = _rope(big[:, 512 + h * LANES:512 + (h + 1) * LANES], dcos, dsin).astype(BF16)
    vd_ref[...] = big[:, 1024:1536].astype(BF16)

    cq = big[:, 1536:1792]
    cqn = _rms(cq, qn_ref[...], MLA_Q_RANK).astype(BF16)
    q = _dot(cqn, wuq_ref[...])
    ckv = big[:, 1792:1920]
    ckvn = _rms(ckv, kvn_ref[...], MLA_KV_RANK).astype(BF16)
    kv = _dot(ckvn, wukv_ref[...])
    kr = _rope(big[:, 1920:2048], mcos, msin)
    for h in range(MLA_HEADS):
        sl = slice(h * LANES, (h + 1) * LANES)
        qm_ref[:, sl] = (_rope(q[:, sl], mcos, msin) * q_scale_mla).astype(BF16)
        km_ref[:, sl] = (kv[:, sl] + kr).astype(BF16)
    vm_ref[...] = kv[:, 1024:1536].astype(BF16)


def _even_proj(x2, g, wbig, qn, wuq, kvn, wukv, tabs, q_scale_mla, q_scale_diff):
    n = x2.shape[0]
    tm = ROW_TILE
    ns = SEQ // tm
    row = lambda w: pl.BlockSpec((tm, w), lambda i: (i, 0))
    full = lambda a: pl.BlockSpec(a.shape, lambda i: (0, 0), pipeline_mode=pl.Buffered(1))
    tab = pl.BlockSpec((tm, LANES), lambda i: (i % ns, 0))
    dcos, dsin, mcos, msin = tabs
    out_w = (1024, 1024, 512, 512, 512, 512)
    return pl.pallas_call(
        functools.partial(_even_proj_kernel, q_scale_mla=q_scale_mla, q_scale_diff=q_scale_diff),
        grid=(n // tm,),
        in_specs=[row(D_MODEL), full(g), full(wbig), full(qn), full(wuq), full(kvn), full(wukv), tab, tab, tab, tab],
        out_specs=[row(w) for w in out_w],
        out_shape=[jax.ShapeDtypeStruct((n, w), BF16) for w in out_w],
        compiler_params=pltpu.CompilerParams(dimension_semantics=("parallel",), vmem_limit_bytes=VMEM_LIMIT),
        name="even_proj",
    )(x2, g, wbig, qn, wuq, kvn, wukv, dcos, dsin, mcos, msin)


def _flash_rows(q, k_ref, v_ref, kcol, vcol):
    rows = q.shape[0]
    ones = jnp.ones((KV_TILE, LANES), BF16)
    m = jnp.full((rows, 1), NEG_INIT, F32)
    acc = jnp.zeros((rows, 2 * LANES), F32)
    for start in range(0, k_ref.shape[0], KV_TILE):
        ks = k_ref[start:start + KV_TILE, kcol:kcol + LANES]
        vs = jnp.concatenate([v_ref[start:start + KV_TILE, vcol:vcol + LANES], ones], axis=1)
        s = lax.dot_general(q, ks, (((1,), (1,)), ((), ())), preferred_element_type=F32)
        m_new = jnp.maximum(m, jnp.max(s, axis=-1, keepdims=True))
        alpha = jnp.exp2(m - m_new)
        p = jnp.exp2(s - m_new)
        acc = alpha * acc + _dot(p.astype(BF16), vs)
        m = m_new
    return acc[:, 0:LANES], acc[:, LANES:2 * LANES]


def _mla_attn_kernel(q_ref, k_ref, v_ref, o_ref):
    acc0, l0 = _flash_rows(q_ref[:, 0:LANES], k_ref, v_ref, 0, 0)
    acc1, l1 = _flash_rows(q_ref[:, LANES:2 * LANES], k_ref, v_ref, LANES, 0)
    lane = lax.broadcasted_iota(jnp.int32, acc0.shape, 1)
    o = jnp.where(lane < MLA_V, acc0 * (1.0 / l0), acc1 * (1.0 / l1))
    o_ref[...] = o.astype(BF16)


def _diff_attn_kernel(q_ref, k_ref, v_ref, lq1_ref, lk1_ref, lq2_ref, lk2_ref, sub_ref, o_ref, *, lam_init):
    q = q_ref[...]
    tq = q.shape[0]
    lane = lax.broadcasted_iota(jnp.int32, q.shape, 1)
    comp0 = (lane % HALF) < (HALF // 2)
    zero = jnp.zeros_like(q)
    q2 = jnp.concatenate([jnp.where(comp0, q, zero), jnp.where(comp0, zero, q)], axis=0)
    acc, l = _flash_rows(q2, k_ref, v_ref, 0, 0)
    o = acc * (1.0 / l)
    lam = (jnp.exp(jnp.sum(lq1_ref[...] * lk1_ref[...], axis=-1, keepdims=True))
           - jnp.exp(jnp.sum(lq2_ref[...] * lk2_ref[...], axis=-1, keepdims=True)) + lam_init)
    od = o[:tq] - lam * o[tq:]
    od = _rms(od, sub_ref[...], LANES) * (1.0 - lam_init)
    o_ref[...] = od.astype(BF16)


def _gqa_attn_kernel(q_ref, k_ref, v_ref, o_ref):
    tq = q_ref.shape[0]
    q = jnp.concatenate([q_ref[:, g * LANES:(g + 1) * LANES] for g in range(GQA_GROUP)], axis=0)
    acc, l = _flash_rows(q, k_ref, v_ref, 0, 0)
    o = acc * (1.0 / l)
    for g in range(GQA_GROUP):
        o_ref[:, g * LANES:(g + 1) * LANES] = o[g * tq:(g + 1) * tq].astype(BF16)


def _attention(body, q, k, v, extra, *, q_w, k_w, v_w, o_w, tq, n_groups, name):
    n = q.shape[0]
    b = n // SEQ
    nq = SEQ // tq
    in_specs = [
        pl.BlockSpec((tq, q_w), lambda bi, g, i: (bi * nq + i, g)),
        pl.BlockSpec((SEQ, k_w), lambda bi, g, i: (bi, g)),
        pl.BlockSpec((SEQ, v_w), lambda bi, g, i: (bi, g)),
    ] + [pl.BlockSpec(a.shape, lambda bi, g, i: (0, 0)) for a in extra]
    return pl.pallas_call(
        body,
        grid=(b, n_groups, nq),
        in_specs=in_specs,
        out_specs=pl.BlockSpec((tq, o_w), lambda bi, g, i: (bi * nq + i, g)),
        out_shape=jax.ShapeDtypeStruct((n, n_groups * o_w), BF16),
        compiler_params=pltpu.CompilerParams(
            dimension_semantics=("parallel", "parallel", "parallel"), vmem_limit_bytes=VMEM_LIMIT),
        name=name,
    )(q, k, v, *extra)


def _post_kernel(*refs, n_mix, final):
    h_ref = refs[0]
    o_refs = refs[1:1 + n_mix]
    w_refs = refs[1 + n_mix:1 + 2 * n_mix]
    g_ref, wg_ref, wu_ref, wd_ref = refs[1 + 2 * n_mix:5 + 2 * n_mix]
    rest = refs[5 + 2 * n_mix:]
    if final:
        fg_ref, out_ref = rest
    else:
        (out_ref,) = rest
    h = h_ref[...]
    for o_ref, w_ref in zip(o_refs, w_refs):
        h = h + _dot(o_ref[...], w_ref[...])
    hn = _rms(h, g_ref[...], D_MODEL).astype(BF16)
    gate = _dot(hn, wg_ref[...])
    up = _dot(hn, wu_ref[...])
    act = (gate * (1.0 / (1.0 + jnp.exp(-gate))) * up).astype(BF16)
    h = h + _dot(act, wd_ref[...])
    if final:
        h = _rms(h, fg_ref[...], D_MODEL)
    out_ref[...] = h


def _post(h2, mixes, w_outs, g, wg, wu, wd, final_gain=None):
    n = h2.shape[0]
    tm = ROW_TILE
    row = lambda w: pl.BlockSpec((tm, w), lambda i: (i, 0))
    const = lambda a: pl.BlockSpec(a.shape, lambda i: (0, 0), pipeline_mode=pl.Buffered(1))
    final = final_gain is not None
    args = [h2, *mixes, *w_outs, g, wg, wu, wd] + ([final_gain] if final else [])
    in_specs = ([row(D_MODEL)] + [row(m.shape[1]) for m in mixes] + [const(w) for w in w_outs]
                + [const(g), const(wg), const(wu), const(wd)] + ([const(final_gain)] if final else []))
    return pl.pallas_call(
        functools.partial(_post_kernel, n_mix=len(mixes), final=final),
        grid=(n // tm,),
        in_specs=in_specs,
        out_specs=row(D_MODEL),
        out_shape=jax.ShapeDtypeStruct((n, D_MODEL), F32),
        compiler_params=pltpu.CompilerParams(dimension_semantics=("parallel",), vmem_limit_bytes=VMEM_LIMIT),
        name="post_final" if final else "post",
    )(*args)


def _odd_proj_kernel(h_ref, g_ref, w_ref, qn_ref, kn_ref, cos_ref, sin_ref, q_ref, k_ref, v_ref, *, q_scale):
    qn, kn = qn_ref[...], kn_ref[...]
    for r0 in range(0, h_ref.shape[0], ROW_CHUNK):
        rows = slice(r0, r0 + ROW_CHUNK)
        hn = _rms(h_ref[rows, :], g_ref[...], D_MODEL).astype(BF16)
        proj = _dot(hn, w_ref[...])
        cos, sin = cos_ref[rows, :], sin_ref[rows, :]
        for h in range(GQA_HEADS):
            sl = slice(h * LANES, (h + 1) * LANES)
            q_ref[rows, sl] = (_rope(_rms(proj[:, sl], qn, LANES), cos, sin) * q_scale).astype(BF16)
        for h in range(GQA_KV_HEADS):
            sl = slice(h * LANES, (h + 1) * LANES)
            k_ref[rows, sl] = _rope(_rms(proj[:, 1024 + h * LANES:1024 + (h + 1) * LANES], kn, LANES), cos, sin).astype(BF16)
        v_ref[rows, :] = proj[:, 1280:1536].astype(BF16)


def _odd_proj(h2, g, w, qn, kn, cos, sin, q_scale):
    n = h2.shape[0]
    tm = ODD_ROW_TILE
    ns = SEQ // tm
    row = lambda w_: pl.BlockSpec((tm, w_), lambda i: (i, 0))
    full = lambda a: pl.BlockSpec(a.shape, lambda i: (0, 0), pipeline_mode=pl.Buffered(1))
    tab = pl.BlockSpec((tm, LANES), lambda i: (i % ns, 0))
    out_w = (1024, 256, 256)
    return pl.pallas_call(
        functools.partial(_odd_proj_kernel, q_scale=q_scale),
        grid=(n // tm,),
        in_specs=[row(D_MODEL), full(g), full(w), full(qn), full(kn), tab, tab],
        out_specs=[row(w_) for w_ in out_w],
        out_shape=[jax.ShapeDtypeStruct((n, w_), BF16) for w_ in out_w],
        compiler_params=pltpu.CompilerParams(dimension_semantics=("parallel",), vmem_limit_bytes=VMEM_LIMIT),
        name="odd_proj",
    )(h2, g, w, qn, kn, cos, sin)


def _prep_even_weights(w_in, q_norm, w_uq, kv_norm, w_ukv):
    o_kv = MLA_Q_RANK
    o_kr = o_kv + MLA_KV_RANK
    o_d = o_kr + MLA_ROPE
    w_cq, w_ckv, w_kr = w_in[:, :o_kv], w_in[:, o_kv:o_kr], w_in[:, o_kr:o_d]
    w_qd, w_kd, w_vd = w_in[:, o_d:o_d + 512], w_in[:, o_d + 512:o_d + 1024], w_in[:, o_d + 1024:o_d + 1536]
    slab = lambda w: _cat_slices(w.reshape(D_MODEL, DIFF_HEADS, LANES), _DIFF_PIECES).reshape(D_MODEL, DIFF_HEADS * LANES)
    wbig = jnp.concatenate(
        [slab(w_qd), slab(w_kd), w_vd, w_cq, jnp.zeros((D_MODEL, 256 - MLA_Q_RANK), F32), w_ckv,
         _cat_slices(w_kr, _MLA_KROPE_PIECES)], axis=1).astype(BF16)

    qn = jnp.concatenate([q_norm, jnp.zeros((256 - MLA_Q_RANK,), F32)])[None, :]
    uq = _cat_slices(w_uq.reshape(MLA_Q_RANK, MLA_HEADS, MLA_NOPE + MLA_ROPE), _MLA_QK_PIECES)
    wuq = jnp.concatenate([uq.reshape(MLA_Q_RANK, MLA_HEADS * LANES),
                           jnp.zeros((256 - MLA_Q_RANK, MLA_HEADS * LANES), F32)], axis=0).astype(BF16)

    ukv = w_ukv.reshape(MLA_KV_RANK, MLA_HEADS, MLA_NOPE + MLA_V)
    wk = _cat_slices(ukv[:, :, :MLA_NOPE], _MLA_KNOPE_PIECES)
    wv = ukv[:, :, MLA_NOPE:]
    wukv = jnp.concatenate([wk.reshape(MLA_KV_RANK, -1), wv.reshape(MLA_KV_RANK, -1)], axis=1).astype(BF16)
    return wbig, qn, wuq, kv_norm[None, :], wukv


def _prep_odd_weights(w_qkv, q_norm, k_norm):
    nqk = GQA_HEADS + GQA_KV_HEADS
    qk = _cat_slices(w_qkv[:, :nqk * LANES].reshape(D_MODEL, nqk, LANES), _AXIAL_PIECES).reshape(D_MODEL, nqk * LANES)
    w = jnp.concatenate([qk, w_qkv[:, nqk * LANES:]], axis=1).astype(BF16)
    return w, _cat_slices(q_norm, _AXIAL_PIECES)[None, :], _cat_slices(k_norm, _AXIAL_PIECES)[None, :]


def kernel(x, e_attn_norm, e_w_in, e_q_norm, e_w_uq, e_kv_norm, e_w_ukv, e_lambda_q1, e_lambda_k1, e_lambda_q2, e_lambda_k2, e_subln, e_w_out, o_attn_norm, o_w_qkv, o_q_norm, o_k_norm, o_w_out, ffn_norm, w_gate, w_up, w_down, final_norm):
    b, s, d = x.shape
    assert (s, d) == (SEQ, D_MODEL)
    n = b * s
    log2e = math.log2(math.e)
    dcos, dsin, mcos, msin, acos, asin = _rope_tables()
    h = x.reshape(n, d)

    wbig, qn, wuq, kvn, wukv = _prep_even_weights(e_w_in[0], e_q_norm[0], e_w_uq[0], e_kv_norm[0], e_w_ukv[0])
    qm, km, vm, qd, kd, vd = _even_proj(
        h, e_attn_norm[0][None, :], wbig, qn, wuq, kvn, wukv, (dcos, dsin, mcos, msin),
        q_scale_mla=(MLA_NOPE + MLA_ROPE) ** -0.5 * log2e, q_scale_diff=DIFF_DIM ** -0.5 * log2e)
    o_mla = _attention(_mla_attn_kernel, qm, km, vm, [], q_w=2 * LANES, k_w=2 * LANES, v_w=LANES, o_w=LANES,
                       tq=ATT_ROWS, n_groups=MLA_HEADS // 2, name="mla_attn")
    lam_init = 0.8 - 0.6 * math.exp(-0.3 * 0)
    lam_args = [e_lambda_q1[0][None, :], e_lambda_k1[0][None, :], e_lambda_q2[0][None, :], e_lambda_k2[0][None, :],
                e_subln[0][None, :]]
    o_diff = _attention(functools.partial(_diff_attn_kernel, lam_init=lam_init), qd, kd, vd, lam_args,
                        q_w=LANES, k_w=LANES, v_w=LANES, o_w=LANES, tq=ATT_ROWS // 2, n_groups=DIFF_HEADS,
                        name="diff_attn")
    w_out = e_w_out[0].astype(BF16)
    h = _post(h, [o_mla, o_diff], [w_out[:512], w_out[512:]], ffn_norm[0][None, :],
              w_gate[0].astype(BF16), w_up[0].astype(BF16), w_down[0].astype(BF16))

    wqkv, oqn, okn = _prep_odd_weights(o_w_qkv[0], o_q_norm[0], o_k_norm[0])
    q, k, v = _odd_proj(h, o_attn_norm[0][None, :], wqkv, oqn, okn, acos, asin, q_scale=LANES ** -0.5 * log2e)
    o = _attention(_gqa_attn_kernel, q, k, v, [], q_w=GQA_GROUP * LANES, k_w=LANES, v_w=LANES, o_w=GQA_GROUP * LANES,
                   tq=ATT_ROWS // GQA_GROUP, n_groups=GQA_KV_HEADS, name="gqa_attn")
    out = _post(h, [o], [o_w_out[0].astype(BF16)], ffn_norm[1][None, :],
                w_gate[1].astype(BF16), w_up[1].astype(BF16), w_down[1].astype(BF16), final_gain=final_norm[None, :])
    return out.reshape(b, s, d)
```

```python
import functools
import math

import jax
import jax.numpy as jnp
import numpy as np
from jax import lax
from jax.experimental import pallas as pl
from jax.experimental.pallas import tpu as pltpu

F32 = jnp.float32
BF16 = jnp.bfloat16

D_MODEL = 1024
SEQ = 4096
GRID_W = 64
EPS = 1e-6
ROPE_THETA = 500000.0
AXIAL_THETA = 10000.0

MLA_HEADS = 8
MLA_Q_RANK = 192
MLA_KV_RANK = 128
MLA_NOPE = 64
MLA_ROPE = 32
MLA_V = 64
DIFF_HEADS = 4
DIFF_DIM = 64
DIFF_ROT = 16
GQA_HEADS = 8
GQA_KV_HEADS = 2
GQA_GROUP = 4
FFN_HIDDEN = 2816

LANES = 128
HALF = LANES // 2
VMEM_LIMIT = 56 * 1024 * 1024

ROW_TILE = 512
ATT_ROWS = 2048
SHARED_ROWS = 4096
KV_TILE = 256
ODD_ROW_TILE = 1024
ROW_CHUNK = 256
NEG_INIT = -1e30


def _cat_slices(a, pieces):
    parts = []
    for p in pieces:
        if isinstance(p, int):
            parts.append(jnp.zeros(a.shape[:-1] + (p,), a.dtype))
        else:
            parts.append(a[..., p[0]:p[1]])
    return jnp.concatenate(parts, axis=-1)


_DIFF_PIECES = ((0, 8), (16, 40), (64, 72), (80, 104), (8, 16), (40, 64), (72, 80), (104, 128))
_MLA_QK_PIECES = ((0, 48), (64, 80), (48, 64), 32, (80, 96))
_MLA_KNOPE_PIECES = ((0, 48), 16, (48, 64), 48)
_MLA_KROPE_PIECES = (48, (0, 16), 48, (16, 32))
_AXIAL_PIECES = ((0, 32), (64, 96), (32, 64), (96, 128))


def _rope_tables():
    f32 = np.float32
    pos = np.arange(SEQ, dtype=np.int32).astype(f32)
    row = (np.arange(SEQ, dtype=np.int32) // GRID_W).astype(f32)
    col = (np.arange(SEQ, dtype=np.int32) % GRID_W).astype(f32)

    def cs(p, dim, theta):
        inv = (f32(theta) ** (-np.arange(0, dim, 2, dtype=f32) / f32(dim))).astype(f32)
        ang = (p[:, None] * inv[None, :]).astype(f32)
        return np.cos(ang).astype(f32), np.sin(ang).astype(f32)

    ones = lambda n: np.ones((SEQ, n), f32)
    zeros = lambda n: np.zeros((SEQ, n), f32)

    c, s = cs(pos, DIFF_ROT, ROPE_THETA)
    cblk = np.concatenate([c, ones(24)], axis=1)
    sblk = np.concatenate([s, zeros(24)], axis=1)
    diff_cos = np.concatenate([cblk] * 4, axis=1)
    diff_sin = np.concatenate([-sblk, -sblk, sblk, sblk], axis=1)

    c, s = cs(pos, MLA_ROPE, ROPE_THETA)
    mla_cos = np.concatenate([ones(48), c, ones(48), c], axis=1)
    mla_sin = np.concatenate([zeros(48), -s, zeros(48), s], axis=1)

    cr, sr = cs(row, HALF, AXIAL_THETA)
    cc, sc = cs(col, HALF, AXIAL_THETA)
    ax_cos = np.concatenate([cr, cc, cr, cc], axis=1)
    ax_sin = np.concatenate([-sr, -sc, sr, sc], axis=1)
    return tuple(jnp.asarray(t) for t in (diff_cos, diff_sin, mla_cos, mla_sin, ax_cos, ax_sin))


def _rms(x, gain, n):
    ms = jnp.sum(x * x, axis=-1, keepdims=True) * (1.0 / n)
    return x * lax.rsqrt(ms + EPS) * gain


def _rope(x, cos, sin):
    return x * cos + pltpu.roll(x, HALF, 1) * sin


def _dot(a, b):
    return jnp.dot(a, b, preferred_element_type=F32)


def _even_proj_kernel(x_ref, g_ref, wbig_ref, qn_ref, wuq_ref, kvn_ref, wukv_ref,
                      dcos_ref, dsin_ref, mcos_ref, msin_ref,
                      qm_ref, km_ref, vm_ref, qd_ref, kd_ref, vd_ref, *, q_scale_mla, q_scale_diff):
    xn = _rms(x_ref[...], g_ref[...], D_MODEL).astype(BF16)
    big = _dot(xn, wbig_ref[...])
    dcos, dsin = dcos_ref[...], dsin_ref[...]
    mcos, msin = mcos_ref[...], msin_ref[...]

    for h in range(DIFF_HEADS):
        sl = slice(h * LANES, (h + 1) * LANES)
        qd_ref[:, sl] = (_rope(big[:, sl], dcos, dsin) * q_scale_diff).astype(BF16)
        kd_ref[:, sl] = _rope(big[:, 512 + h * LANES:512 + (h + 1) * LANES], dcos, dsin).astype(BF16)
    vd_ref[...] = big[:, 1024:1536].astype(BF16)

    cq = big[:, 1536:1792]
    cqn = _rms(cq, qn_ref[...], MLA_Q_RANK).astype(BF16)
    q = _dot(cqn, wuq_ref[...])
    ckv = big[:, 1792:1920]
    ckvn = _rms(ckv, kvn_ref[...], MLA_KV_RANK).astype(BF16)
    kv = _dot(ckvn, wukv_ref[...])
    kr = _rope(big[:, 1920:2048], mcos, msin)
    for h in range(MLA_HEADS):
        sl = slice(h * LANES, (h + 1) * LANES)
        qm_ref[:, sl] = (_rope(q[:, sl], mcos, msin) * q_scale_mla).astype(BF16)
        km_ref[:, sl] = (kv[:, sl] + kr).astype(BF16)
    vm_ref[...] = kv[:, 1024:1536].astype(BF16)


def _even_proj(x2, g, wbig, qn, wuq, kvn, wukv, tabs, q_scale_mla, q_scale_diff):
    n = x2.shape[0]
    tm = ROW_TILE
    ns = SEQ // tm
    row = lambda w: pl.BlockSpec((tm, w), lambda i: (i, 0))
    full = lambda a: pl.BlockSpec(a.shape, lambda i: (0, 0), pipeline_mode=pl.Buffered(1))
    tab = pl.BlockSpec((tm, LANES), lambda i: (i % ns, 0))
    dcos, dsin, mcos, msin = tabs
    out_w = (1024, 1024, 512, 512, 512, 512)
    return pl.pallas_call(
        functools.partial(_even_proj_kernel, q_scale_mla=q_scale_mla, q_scale_diff=q_scale_diff),
        grid=(n // tm,),
        in_specs=[row(D_MODEL), full(g), full(wbig), full(qn), full(wuq), full(kvn), full(wukv), tab, tab, tab, tab],
        out_specs=[row(w) for w in out_w],
        out_shape=[jax.ShapeDtypeStruct((n, w), BF16) for w in out_w],
        compiler_params=pltpu.CompilerParams(dimension_semantics=("parallel",), vmem_limit_bytes=VMEM_LIMIT),
        name="even_proj",
    )(x2, g, wbig, qn, wuq, kvn, wukv, dcos, dsin, mcos, msin)


def _flash_rows(q, k_ref, v_ref, kcol, vcol):
    rows = q.shape[0]
    ones = jnp.ones((KV_TILE, LANES), BF16)
    m = jnp.full((rows, 1), NEG_INIT, F32)
    acc = jnp.zeros((rows, 2 * LANES), F32)
    for start in range(0, k_ref.shape[0], KV_TILE):
        ks = k_ref[start:start + KV_TILE, kcol:kcol + LANES]
        vs = jnp.concatenate([v_ref[start:start + KV_TILE, vcol:vcol + LANES], ones], axis=1)
        s = lax.dot_general(q, ks, (((1,), (1,)), ((), ())), preferred_element_type=F32)
        m_new = jnp.maximum(m, jnp.max(s, axis=-1, keepdims=True))
        alpha = jnp.exp2(m - m_new)
        p = jnp.exp2(s - m_new)
        acc = alpha * acc + _dot(p.astype(BF16), vs)
        m = m_new
    return acc[:, 0:LANES], acc[:, LANES:2 * LANES]


def _mla_attn_kernel(q_ref, k_ref, v_ref, o_ref):
    acc0, l0 = _flash_rows(q_ref[:, 0:LANES], k_ref, v_ref, 0, 0)
    acc1, l1 = _flash_rows(q_ref[:, LANES:2 * LANES], k_ref, v_ref, LANES, 0)
    lane = lax.broadcasted_iota(jnp.int32, acc0.shape, 1)
    o = jnp.where(lane < MLA_V, acc0 * (1.0 / l0), acc1 * (1.0 / l1))
    o_ref[...] = o.astype(BF16)


def _diff_attn_kernel(q_ref, k_ref, v_ref, lq1_ref, lk1_ref, lq2_ref, lk2_ref, sub_ref, o_ref, *, lam_init):
    q = q_ref[...]
    tq = q.shape[0]
    lane = lax.broadcasted_iota(jnp.int32, q.shape, 1)
    comp0 = (lane % HALF) < (HALF // 2)
    zero = jnp.zeros_like(q)
    q2 = jnp.concatenate([jnp.where(comp0, q, zero), jnp.where(comp0, zero, q)], axis=0)
    acc, l = _flash_rows(q2, k_ref, v_ref, 0, 0)
    o = acc * (1.0 / l)
    lam = (jnp.exp(jnp.sum(lq1_ref[...] * lk1_ref[...], axis=-1, keepdims=True))
           - jnp.exp(jnp.sum(lq2_ref[...] * lk2_ref[...], axis=-1, keepdims=True)) + lam_init)
    od = o[:tq] - lam * o[tq:]
    od = _rms(od, sub_ref[...], LANES) * (1.0 - lam_init)
    o_ref[...] = od.astype(BF16)


def _gqa_attn_kernel(q_ref, k_ref, v_ref, o_ref):
    tq = q_ref.shape[0]
    q = jnp.concatenate([q_ref[:, g * LANES:(g + 1) * LANES] for g in range(GQA_GROUP)], axis=0)
    acc, l = _flash_rows(q, k_ref, v_ref, 0, 0)
    o = acc * (1.0 / l)
    for g in range(GQA_GROUP):
        o_ref[:, g * LANES:(g + 1) * LANES] = o[g * tq:(g + 1) * tq].astype(BF16)


def _attention(body, q, k, v, extra, *, q_w, k_w, v_w, o_w, tq, n_groups, name):
    n = q.shape[0]
    b = n // SEQ
    nq = SEQ // tq
    in_specs = [
        pl.BlockSpec((tq, q_w), lambda bi, g, i: (bi * nq + i, g)),
        pl.BlockSpec((SEQ, k_w), lambda bi, g, i: (bi, g)),
        pl.BlockSpec((SEQ, v_w), lambda bi, g, i: (bi, g)),
    ] + [pl.BlockSpec(a.shape, lambda bi, g, i: (0, 0)) for a in extra]
    return pl.pallas_call(
        body,
        grid=(b, n_groups, nq),
        in_specs=in_specs,
        out_specs=pl.BlockSpec((tq, o_w), lambda bi, g, i: (bi * nq + i, g)),
        out_shape=jax.ShapeDtypeStruct((n, n_groups * o_w), BF16),
        compiler_params=pltpu.CompilerParams(
            dimension_semantics=("parallel", "parallel", "parallel"), vmem_limit_bytes=VMEM_LIMIT),
        name=name,
    )(q, k, v, *extra)


def _post_kernel(*refs, n_mix, final):
    h_ref = refs[0]
    o_refs = refs[1:1 + n_mix]
    w_refs = refs[1 + n_mix:1 + 2 * n_mix]
    g_ref, wg_ref, wu_ref, wd_ref = refs[1 + 2 * n_mix:5 + 2 * n_mix]
    rest = refs[5 + 2 * n_mix:]
    if final:
        fg_ref, out_ref = rest
    else:
        (out_ref,) = rest
    h = h_ref[...]
    for o_ref, w_ref in zip(o_refs, w_refs):
        h = h + _dot(o_ref[...], w_ref[...])
    hn = _rms(h, g_ref[...], D_MODEL).astype(BF16)
    gate = _dot(hn, wg_ref[...])
    up = _dot(hn, wu_ref[...])
    act = (gate * (1.0 / (1.0 + jnp.exp(-gate))) * up).astype(BF16)
    h = h + _dot(act, wd_ref[...])
    if final:
        h = _rms(h, fg_ref[...], D_MODEL)
    out_ref[...] = h


def _post(h2, mixes, w_outs, g, wg, wu, wd, final_gain=None):
    n = h2.shape[0]
    tm = ROW_TILE
    row = lambda w: pl.BlockSpec((tm, w), lambda i: (i, 0))
    const = lambda a: pl.BlockSpec(a.shape, lambda i: (0, 0), pipeline_mode=pl.Buffered(1))
    final = final_gain is not None
    args = [h2, *mixes, *w_outs, g, wg, wu, wd] + ([final_gain] if final else [])
    in_specs = ([row(D_MODEL)] + [row(m.shape[1]) for m in mixes] + [const(w) for w in w_outs]
                + [const(g), const(wg), const(wu), const(wd)] + ([const(final_gain)] if final else []))
    return pl.pallas_call(
        functools.partial(_post_kernel, n_mix=len(mixes), final=final),
        grid=(n // tm,),
        in_specs=in_specs,
        out_specs=row(D_MODEL),
        out_shape=jax.ShapeDtypeStruct((n, D_MODEL), F32),
        compiler_params=pltpu.CompilerParams(dimension_semantics=("parallel",), vmem_limit_bytes=VMEM_LIMIT),
        name="post_final" if final else "post",
    )(*args)


def _odd_proj_kernel(h_ref, g_ref, w_ref, qn_ref, kn_ref, cos_ref, sin_ref, q_ref, k_ref, v_ref, *, q_scale):
    qn, kn = qn_ref[...], kn_ref[...]
    for r0 in range(0, h_ref.shape[0], ROW_CHUNK):
        rows = slice(r0, r0 + ROW_CHUNK)
        hn = _rms(h_ref[rows, :], g_ref[...], D_MODEL).astype(BF16)
        proj = _dot(hn, w_ref[...])
        cos, sin = cos_ref[rows, :], sin_ref[rows, :]
        for h in range(GQA_HEADS):
            sl = slice(h * LANES, (h + 1) * LANES)
            q_ref[rows, sl] = (_rope(_rms(proj[:, sl], qn, LANES), cos, sin) * q_scale).astype(BF16)
        for h in range(GQA_KV_HEADS):
            sl = slice(h * LANES, (h + 1) * LANES)
            k_ref[rows, sl] = _rope(_rms(proj[:, 1024 + h * LANES:1024 + (h + 1) * LANES], kn, LANES), cos, sin).astype(BF16)
        v_ref[rows, :] = proj[:, 1280:1536].astype(BF16)


def _odd_proj(h2, g, w, qn, kn, cos, sin, q_scale):
    n = h2.shape[0]
    tm = ODD_ROW_TILE
    ns = SEQ // tm
    row = lambda w_: pl.BlockSpec((tm, w_), lambda i: (i, 0))
    full = lambda a: pl.BlockSpec(a.shape, lambda i: (0, 0), pipeline_mode=pl.Buffered(1))
    tab = pl.BlockSpec((tm, LANES), lambda i: (i % ns, 0))
    out_w = (1024, 256, 256)
    return pl.pallas_call(
        functools.partial(_odd_proj_kernel, q_scale=q_scale),
        grid=(n // tm,),
        in_specs=[row(D_MODEL), full(g), full(w), full(qn), full(kn), tab, tab],
        out_specs=[row(w_) for w_ in out_w],
        out_shape=[jax.ShapeDtypeStruct((n, w_), BF16) for w_ in out_w],
        compiler_params=pltpu.CompilerParams(dimension_semantics=("parallel",), vmem_limit_bytes=VMEM_LIMIT),
        name="odd_proj",
    )(h2, g, w, qn, kn, cos, sin)


def _prep_even_weights(w_in, q_norm, w_uq, kv_norm, w_ukv):
    o_kv = MLA_Q_RANK
    o_kr = o_kv + MLA_KV_RANK
    o_d = o_kr + MLA_ROPE
    w_cq, w_ckv, w_kr = w_in[:, :o_kv], w_in[:, o_kv:o_kr], w_in[:, o_kr:o_d]
    w_qd, w_kd, w_vd = w_in[:, o_d:o_d + 512], w_in[:, o_d + 512:o_d + 1024], w_in[:, o_d + 1024:o_d + 1536]
    slab = lambda w: _cat_slices(w.reshape(D_MODEL, DIFF_HEADS, LANES), _DIFF_PIECES).reshape(D_MODEL, DIFF_HEADS * LANES)
    wbig = jnp.concatenate(
        [slab(w_qd), slab(w_kd), w_vd, w_cq, jnp.zeros((D_MODEL, 256 - MLA_Q_RANK), F32), w_ckv,
         _cat_slices(w_kr, _MLA_KROPE_PIECES)], axis=1).astype(BF16)

    qn = jnp.concatenate([q_norm, jnp.zeros((256 - MLA_Q_RANK,), F32)])[None, :]
    uq = _cat_slices(w_uq.reshape(MLA_Q_RANK, MLA_HEADS, MLA_NOPE + MLA_ROPE), _MLA_QK_PIECES)
    wuq = jnp.concatenate([uq.reshape(MLA_Q_RANK, MLA_HEADS * LANES),
                           jnp.zeros((256 - MLA_Q_RANK, MLA_HEADS * LANES), F32)], axis=0).astype(BF16)

    ukv = w_ukv.reshape(MLA_KV_RANK, MLA_HEADS, MLA_NOPE + MLA_V)
    wk = _cat_slices(ukv[:, :, :MLA_NOPE], _MLA_KNOPE_PIECES)
    wv = ukv[:, :, MLA_NOPE:]
    wukv = jnp.concatenate([wk.reshape(MLA_KV_RANK, -1), wv.reshape(MLA_KV_RANK, -1)], axis=1).astype(BF16)
    return wbig, qn, wuq, kv_norm[None, :], wukv


def _prep_odd_weights(w_qkv, q_norm, k_norm):
    nqk = GQA_HEADS + GQA_KV_HEADS
    qk = _cat_slices(w_qkv[:, :nqk * LANES].reshape(D_MODEL, nqk, LANES), _AXIAL_PIECES).reshape(D_MODEL, nqk * LANES)
    w = jnp.concatenate([qk, w_qkv[:, nqk * LANES:]], axis=1).astype(BF16)
    return w, _cat_slices(q_norm, _AXIAL_PIECES)[None, :], _cat_slices(k_norm, _AXIAL_PIECES)[None, :]


def kernel(x, e_attn_norm, e_w_in, e_q_norm, e_w_uq, e_kv_norm, e_w_ukv, e_lambda_q1, e_lambda_k1, e_lambda_q2, e_lambda_k2, e_subln, e_w_out, o_attn_norm, o_w_qkv, o_q_norm, o_k_norm, o_w_out, ffn_norm, w_gate, w_up, w_down, final_norm):
    b, s, d = x.shape
    assert (s, d) == (SEQ, D_MODEL)
    n = b * s
    log2e = math.log2(math.e)
    dcos, dsin, mcos, msin, acos, asin = _rope_tables()
    h = x.reshape(n, d)

    wbig, qn, wuq, kvn, wukv = _prep_even_weights(e_w_in[0], e_q_norm[0], e_w_uq[0], e_kv_norm[0], e_w_ukv[0])
    qm, km, vm, qd, kd, vd = _even_proj(
        h, e_attn_norm[0][None, :], wbig, qn, wuq, kvn, wukv, (dcos, dsin, mcos, msin),
        q_scale_mla=(MLA_NOPE + MLA_ROPE) ** -0.5 * log2e, q_scale_diff=DIFF_DIM ** -0.5 * log2e)
    o_mla = _attention(_mla_attn_kernel, qm, km, vm, [], q_w=2 * LANES, k_w=2 * LANES, v_w=LANES, o_w=LANES,
                       tq=ATT_ROWS, n_groups=MLA_HEADS // 2, name="mla_attn")
    lam_init = 0.8 - 0.6 * math.exp(-0.3 * 0)
    lam_args = [e_lambda_q1[0][None, :], e_lambda_k1[0][None, :], e_lambda_q2[0][None, :], e_lambda_k2[0][None, :],
                e_subln[0][None, :]]
    o_diff = _attention(functools.partial(_diff_attn_kernel, lam_init=lam_init), qd, kd, vd, lam_args,
                        q_w=LANES, k_w=LANES, v_w=LANES, o_w=LANES, tq=SHARED_ROWS // 2, n_groups=DIFF_HEADS,
                        name="diff_attn")
    w_out = e_w_out[0].astype(BF16)
    h = _post(h, [o_mla, o_diff], [w_out[:512], w_out[512:]], ffn_norm[0][None, :],
              w_gate[0].astype(BF16), w_up[0].astype(BF16), w_down[0].astype(BF16))

    wqkv, oqn, okn = _prep_odd_weights(o_w_qkv[0], o_q_norm[0], o_k_norm[0])
    q, k, v = _odd_proj(h, o_attn_norm[0][None, :], wqkv, oqn, okn, acos, asin, q_scale=LANES ** -0.5 * log2e)
    o = _attention(_gqa_attn_kernel, q, k, v, [], q_w=GQA_GROUP * LANES, k_w=LANES, v_w=LANES, o_w=GQA_GROUP * LANES,
                   tq=SHARED_ROWS // GQA_GROUP, n_groups=GQA_KV_HEADS, name="gqa_attn")
    out = _post(h, [o], [o_w_out[0].astype(BF16)], ffn_norm[1][None, :],
                w_gate[1].astype(BF16), w_up[1].astype(BF16), w_down[1].astype(BF16), final_gain=final_norm[None, :])
    return out.reshape(b, s, d)
```

```python
import functools
import math

import jax
import jax.numpy as jnp
import numpy as np
from jax import lax
from jax.experimental import pallas as pl
from jax.experimental.pallas import tpu as pltpu

F32 = jnp.float32
BF16 = jnp.bfloat16

D_MODEL = 1024
SEQ = 4096
GRID_W = 64
EPS = 1e-6
ROPE_THETA = 500000.0
AXIAL_THETA = 10000.0

MLA_HEADS = 8
MLA_Q_RANK = 192
MLA_KV_RANK = 128
MLA_NOPE = 64
MLA_ROPE = 32
MLA_V = 64
DIFF_HEADS = 4
DIFF_DIM = 64
DIFF_ROT = 16
GQA_HEADS = 8
GQA_KV_HEADS = 2
GQA_GROUP = 4
FFN_HIDDEN = 2816

LANES = 128
HALF = LANES // 2
VMEM_LIMIT = 56 * 1024 * 1024

ROW_TILE = 512
ATT_ROWS = 2048
SHARED_ROWS = 4096
KV_TILE = 256
ODD_ROW_TILE = 1024
ROW_CHUNK = 256
NEG_INIT = -1e30


def _cat_slices(a, pieces):
    parts = []
    for p in pieces:
        if isinstance(p, int):
            parts.append(jnp.zeros(a.shape[:-1] + (p,), a.dtype))
        else:
            parts.append(a[..., p[0]:p[1]])
    return jnp.concatenate(parts, axis=-1)


_DIFF_PIECES = ((0, 8), (16, 40), (64, 72), (80, 104), (8, 16), (40, 64), (72, 80), (104, 128))
_MLA_QK_PIECES = ((0, 48), (64, 80), (48, 64), 32, (80, 96))
_MLA_KNOPE_PIECES = ((0, 48), 16, (48, 64), 48)
_MLA_KROPE_PIECES = (48, (0, 16), 48, (16, 32))
_AXIAL_PIECES = ((0, 32), (64, 96), (32, 64), (96, 128))


def _rope_tables():
    f32 = np.float32
    pos = np.arange(SEQ, dtype=np.int32).astype(f32)
    row = (np.arange(SEQ, dtype=np.int32) // GRID_W).astype(f32)
    col = (np.arange(SEQ, dtype=np.int32) % GRID_W).astype(f32)

    def cs(p, dim, theta):
        inv = (f32(theta) ** (-np.arange(0, dim, 2, dtype=f32) / f32(dim))).astype(f32)
        ang = (p[:, None] * inv[None, :]).astype(f32)
        return np.cos(ang).astype(f32), np.sin(ang).astype(f32)

    ones = lambda n: np.ones((SEQ, n), f32)
    zeros = lambda n: np.zeros((SEQ, n), f32)

    c, s = cs(pos, DIFF_ROT, ROPE_THETA)
    cblk = np.concatenate([c, ones(24)], axis=1)
    sblk = np.concatenate([s, zeros(24)], axis=1)
    diff_cos = np.concatenate([cblk] * 4, axis=1)
    diff_sin = np.concatenate([-sblk, -sblk, sblk, sblk], axis=1)

    c, s = cs(pos, MLA_ROPE, ROPE_THETA)
    mla_cos = np.concatenate([ones(48), c, ones(48), c], axis=1)
    mla_sin = np.concatenate([zeros(48), -s, zeros(48), s], axis=1)

    cr, sr = cs(row, HALF, AXIAL_THETA)
    cc, sc = cs(col, HALF, AXIAL_THETA)
    ax_cos = np.concatenate([cr, cc, cr, cc], axis=1)
    ax_sin = np.concatenate([-sr, -sc, sr, sc], axis=1)
    return tuple(jnp.asarray(t) for t in (diff_cos, diff_sin, mla_cos, mla_sin, ax_cos, ax_sin))


def _rms(x, gain, n):
    ms = jnp.sum(x * x, axis=-1, keepdims=True) * (1.0 / n)
    return x * lax.rsqrt(ms + EPS) * gain


def _rope(x, cos, sin):
    return x * cos + pltpu.roll(x, HALF, 1) * sin


def _dot(a, b):
    return jnp.dot(a, b, preferred_element_type=F32)


def _even_proj_kernel(x_ref, g_ref, wbig_ref, qn_ref, wuq_ref, kvn_ref, wukv_ref,
                      dcos_ref, dsin_ref, mcos_ref, msin_ref,
                      qm_ref, km_ref, vm_ref, qd_ref, kd_ref, vd_ref, *, q_scale_mla, q_scale_diff):
    xn = _rms(x_ref[...], g_ref[...], D_MODEL).astype(BF16)
    big = _dot(xn, wbig_ref[...])
    dcos, dsin = dcos_ref[...], dsin_ref[...]
    mcos, msin = mcos_ref[...], msin_ref[...]

    for h in range(DIFF_HEADS):
        sl = slice(h * LANES, (h + 1) * LANES)
        qd_ref[:, sl] = (_rope(big[:, sl], dcos, dsin) * q_scale_diff).astype(BF16)
        kd_ref[:, sl] = _rope(big[:, 512 + h * LANES:512 + (h + 1) * LANES], dcos, dsin).astype(BF16)
    vd_ref[...] = big[:, 1024:1536].astype(BF16)

    cq = big[:, 1536:1792]
    cqn = _rms(cq, qn_ref[...], MLA_Q_RANK).astype(BF16)
    q = _dot(cqn, wuq_ref[...])
    ckv = big[:, 1792:1920]
    ckvn = _rms(ckv, kvn_ref[...], MLA_KV_RANK).astype(BF16)
    kv = _dot(ckvn, wukv_ref[...])
    kr = _rope(big[:, 1920:2048], mcos, msin)
    for h in range(MLA_HEADS):
        sl = slice(h * LANES, (h + 1) * LANES)
        qm_ref[:, sl] = (_rope(q[:, sl], mcos, msin) * q_scale_mla).astype(BF16)
        km_ref[:, sl] = (kv[:, sl] + kr).astype(BF16)
    vm_ref[...] = kv[:, 1024:1536].astype(BF16)


def _even_proj(x2, g, wbig, qn, wuq, kvn, wukv, tabs, q_scale_mla, q_scale_diff):
    n = x2.shape[0]
    tm = ROW_TILE
    ns = SEQ // tm
    row = lambda w: pl.BlockSpec((tm, w), lambda i: (i, 0))
    full = lambda a: pl.BlockSpec(a.shape, lambda i: (0, 0), pipeline_mode=pl.Buffered(1))
    tab = pl.BlockSpec((tm, LANES), lambda i: (i % ns, 0))
    dcos, dsin, mcos, msin = tabs
    out_w = (1024, 1024, 512, 512, 512, 512)
    return pl.pallas_call(
        functools.partial(_even_proj_kernel, q_scale_mla=q_scale_mla, q_scale_diff=q_scale_diff),
        grid=(n // tm,),
        in_specs=[row(D_MODEL), full(g), full(wbig), full(qn), full(wuq), full(kvn), full(wukv), tab, tab, tab, tab],
        out_specs=[row(w) for w in out_w],
        out_shape=[jax.ShapeDtypeStruct((n, w), BF16) for w in out_w],
        compiler_params=pltpu.CompilerParams(dimension_semantics=("parallel",), vmem_limit_bytes=VMEM_LIMIT),
        name="even_proj",
    )(x2, g, wbig, qn, wuq, kvn, wukv, dcos, dsin, mcos, msin)


def _flash_rows(qs, k_ref, v_ref, kcols, vcol):
    rows = sum(q.shape[0] for q in qs)
    ones = jnp.ones((KV_TILE, LANES), BF16)
    m = jnp.full((rows, 1), NEG_INIT, F32)
    acc = jnp.zeros((rows, 2 * LANES), F32)
    for start in range(0, k_ref.shape[0], KV_TILE):
        vs = jnp.concatenate([v_ref[start:start + KV_TILE, vcol:vcol + LANES], ones], axis=1)
        s = jnp.concatenate(
            [lax.dot_general(q, k_ref[start:start + KV_TILE, kc:kc + LANES], (((1,), (1,)), ((), ())),
                             preferred_element_type=F32) for q, kc in zip(qs, kcols)], axis=0)
        m_new = jnp.maximum(m, jnp.max(s, axis=-1, keepdims=True))
        alpha = jnp.exp2(m - m_new)
        p = jnp.exp2(s - m_new)
        acc = alpha * acc + _dot(p.astype(BF16), vs)
        m = m_new
    return acc[:, 0:LANES], acc[:, LANES:2 * LANES]


def _mla_attn_kernel(q_ref, k_ref, v_ref, o_ref):
    tq = q_ref.shape[0]
    acc, l = _flash_rows([q_ref[:, 0:LANES], q_ref[:, LANES:2 * LANES]], k_ref, v_ref, [0, LANES], 0)
    o = acc * (1.0 / l)
    lane = lax.broadcasted_iota(jnp.int32, (tq, LANES), 1)
    o_ref[...] = jnp.where(lane < MLA_V, o[:tq], o[tq:]).astype(BF16)


def _diff_attn_kernel(q_ref, k_ref, v_ref, lq1_ref, lk1_ref, lq2_ref, lk2_ref, sub_ref, o_ref, *, lam_init):
    q = q_ref[...]
    tq = q.shape[0]
    lane = lax.broadcasted_iota(jnp.int32, q.shape, 1)
    comp0 = (lane % HALF) < (HALF // 2)
    zero = jnp.zeros_like(q)
    q2 = jnp.concatenate([jnp.where(comp0, q, zero), jnp.where(comp0, zero, q)], axis=0)
    acc, l = _flash_rows([q2], k_ref, v_ref, [0], 0)
    o = acc * (1.0 / l)
    lam = (jnp.exp(jnp.sum(lq1_ref[...] * lk1_ref[...], axis=-1, keepdims=True))
           - jnp.exp(jnp.sum(lq2_ref[...] * lk2_ref[...], axis=-1, keepdims=True)) + lam_init)
    od = o[:tq] - lam * o[tq:]
    od = _rms(od, sub_ref[...], LANES) * (1.0 - lam_init)
    o_ref[...] = od.astype(BF16)


def _gqa_attn_kernel(q_ref, k_ref, v_ref, o_ref):
    tq = q_ref.shape[0]
    q = jnp.concatenate([q_ref[:, g * LANES:(g + 1) * LANES] for g in range(GQA_GROUP)], axis=0)
    acc, l = _flash_rows([q], k_ref, v_ref, [0], 0)
    o = acc * (1.0 / l)
    for g in range(GQA_GROUP):
        o_ref[:, g * LANES:(g + 1) * LANES] = o[g * tq:(g + 1) * tq].astype(BF16)


def _attention(body, q, k, v, extra, *, q_w, k_w, v_w, o_w, tq, n_groups, name):
    n = q.shape[0]
    b = n // SEQ
    nq = SEQ // tq
    in_specs = [
        pl.BlockSpec((tq, q_w), lambda bi, g, i: (bi * nq + i, g)),
        pl.BlockSpec((SEQ, k_w), lambda bi, g, i: (bi, g)),
        pl.BlockSpec((SEQ, v_w), lambda bi, g, i: (bi, g)),
    ] + [pl.BlockSpec(a.shape, lambda bi, g, i: (0, 0)) for a in extra]
    return pl.pallas_call(
        body,
        grid=(b, n_groups, nq),
        in_specs=in_specs,
        out_specs=pl.BlockSpec((tq, o_w), lambda bi, g, i: (bi * nq + i, g)),
        out_shape=jax.ShapeDtypeStruct((n, n_groups * o_w), BF16),
        compiler_params=pltpu.CompilerParams(
            dimension_semantics=("parallel", "parallel", "parallel"), vmem_limit_bytes=VMEM_LIMIT),
        name=name,
    )(q, k, v, *extra)


def _post_kernel(*refs, n_mix, final):
    h_ref = refs[0]
    o_refs = refs[1:1 + n_mix]
    w_refs = refs[1 + n_mix:1 + 2 * n_mix]
    g_ref, wg_ref, wu_ref, wd_ref = refs[1 + 2 * n_mix:5 + 2 * n_mix]
    rest = refs[5 + 2 * n_mix:]
    if final:
        fg_ref, out_ref = rest
    else:
        (out_ref,) = rest
    h = h_ref[...]
    for o_ref, w_ref in zip(o_refs, w_refs):
        h = h + _dot(o_ref[...], w_ref[...])
    hn = _rms(h, g_ref[...], D_MODEL).astype(BF16)
    gate = _dot(hn, wg_ref[...])
    up = _dot(hn, wu_ref[...])
    act = (gate * (1.0 / (1.0 + jnp.exp(-gate))) * up).astype(BF16)
    h = h + _dot(act, wd_ref[...])
    if final:
        h = _rms(h, fg_ref[...], D_MODEL)
    out_ref[...] = h


def _post(h2, mixes, w_outs, g, wg, wu, wd, final_gain=None):
    n = h2.shape[0]
    tm = ROW_TILE
    row = lambda w: pl.BlockSpec((tm, w), lambda i: (i, 0))
    const = lambda a: pl.BlockSpec(a.shape, lambda i: (0, 0), pipeline_mode=pl.Buffered(1))
    final = final_gain is not None
    args = [h2, *mixes, *w_outs, g, wg, wu, wd] + ([final_gain] if final else [])
    in_specs = ([row(D_MODEL)] + [row(m.shape[1]) for m in mixes] + [const(w) for w in w_outs]
                + [const(g), const(wg), const(wu), const(wd)] + ([const(final_gain)] if final else []))
    return pl.pallas_call(
        functools.partial(_post_kernel, n_mix=len(mixes), final=final),
        grid=(n // tm,),
        in_specs=in_specs,
        out_specs=row(D_MODEL),
        out_shape=jax.ShapeDtypeStruct((n, D_MODEL), F32),
        compiler_params=pltpu.CompilerParams(dimension_semantics=("parallel",), vmem_limit_bytes=VMEM_LIMIT),
        name="post_final" if final else "post",
    )(*args)


def _odd_proj_kernel(h_ref, g_ref, w_ref, qn_ref, kn_ref, cos_ref, sin_ref, q_ref, k_ref, v_ref, *, q_scale):
    qn, kn = qn_ref[...], kn_ref[...]
    for r0 in range(0, h_ref.shape[0], ROW_CHUNK):
        rows = slice(r0, r0 + ROW_CHUNK)
        hn = _rms(h_ref[rows, :], g_ref[...], D_MODEL).astype(BF16)
        proj = _dot(hn, w_ref[...])
        cos, sin = cos_ref[rows, :], sin_ref[rows, :]
        for h in range(GQA_HEADS):
            sl = slice(h * LANES, (h + 1) * LANES)
            q_ref[rows, sl] = (_rope(_rms(proj[:, sl], qn, LANES), cos, sin) * q_scale).astype(BF16)
        for h in range(GQA_KV_HEADS):
            sl = slice(h * LANES, (h + 1) * LANES)
            k_ref[rows, sl] = _rope(_rms(proj[:, 1024 + h * LANES:1024 + (h + 1) * LANES], kn, LANES), cos, sin).astype(BF16)
        v_ref[rows, :] = proj[:, 1280:1536].astype(BF16)


def _odd_proj(h2, g, w, qn, kn, cos, sin, q_scale):
    n = h2.shape[0]
    tm = ODD_ROW_TILE
    ns = SEQ // tm
    row = lambda w_: pl.BlockSpec((tm, w_), lambda i: (i, 0))
    full = lambda a: pl.BlockSpec(a.shape, lambda i: (0, 0), pipeline_mode=pl.Buffered(1))
    tab = pl.BlockSpec((tm, LANES), lambda i: (i % ns, 0))
    out_w = (1024, 256, 256)
    return pl.pallas_call(
        functools.partial(_odd_proj_kernel, q_scale=q_scale),
        grid=(n // tm,),
        in_specs=[row(D_MODEL), full(g), full(w), full(qn), full(kn), tab, tab],
        out_specs=[row(w_) for w_ in out_w],
        out_shape=[jax.ShapeDtypeStruct((n, w_), BF16) for w_ in out_w],
        compiler_params=pltpu.CompilerParams(dimension_semantics=("parallel",), vmem_limit_bytes=VMEM_LIMIT),
        name="odd_proj",
    )(h2, g, w, qn, kn, cos, sin)


def _prep_even_weights(w_in, q_norm, w_uq, kv_norm, w_ukv):
    o_kv = MLA_Q_RANK
    o_kr = o_kv + MLA_KV_RANK
    o_d = o_kr + MLA_ROPE
    w_cq, w_ckv, w_kr = w_in[:, :o_kv], w_in[:, o_kv:o_kr], w_in[:, o_kr:o_d]
    w_qd, w_kd, w_vd = w_in[:, o_d:o_d + 512], w_in[:, o_d + 512:o_d + 1024], w_in[:, o_d + 1024:o_d + 1536]
    slab = lambda w: _cat_slices(w.reshape(D_MODEL, DIFF_HEADS, LANES), _DIFF_PIECES).reshape(D_MODEL, DIFF_HEADS * LANES)
    wbig = jnp.concatenate(
        [slab(w_qd), slab(w_kd), w_vd, w_cq, jnp.zeros((D_MODEL, 256 - MLA_Q_RANK), F32), w_ckv,
         _cat_slices(w_kr, _MLA_KROPE_PIECES)], axis=1).astype(BF16)

    qn = jnp.concatenate([q_norm, jnp.zeros((256 - MLA_Q_RANK,), F32)])[None, :]
    uq = _cat_slices(w_uq.reshape(MLA_Q_RANK, MLA_HEADS, MLA_NOPE + MLA_ROPE), _MLA_QK_PIECES)
    wuq = jnp.concatenate([uq.reshape(MLA_Q_RANK, MLA_HEADS * LANES),
                           jnp.zeros((256 - MLA_Q_RANK, MLA_HEADS * LANES), F32)], axis=0).astype(BF16)

    ukv = w_ukv.reshape(MLA_KV_RANK, MLA_HEADS, MLA_NOPE + MLA_V)
    wk = _cat_slices(ukv[:, :, :MLA_NOPE], _MLA_KNOPE_PIECES)
    wv = ukv[:, :, MLA_NOPE:]
    wukv = jnp.concatenate([wk.reshape(MLA_KV_RANK, -1), wv.reshape(MLA_KV_RANK, -1)], axis=1).astype(BF16)
    return wbig, qn, wuq, kv_norm[None, :], wukv


def _prep_odd_weights(w_qkv, q_norm, k_norm):
    nqk = GQA_HEADS + GQA_KV_HEADS
    qk = _cat_slices(w_qkv[:, :nqk * LANES].reshape(D_MODEL, nqk, LANES), _AXIAL_PIECES).reshape(D_MODEL, nqk * LANES)
    w = jnp.concatenate([qk, w_qkv[:, nqk * LANES:]], axis=1).astype(BF16)
    return w, _cat_slices(q_norm, _AXIAL_PIECES)[None, :], _cat_slices(k_norm, _AXIAL_PIECES)[None, :]


def kernel(x, e_attn_norm, e_w_in, e_q_norm, e_w_uq, e_kv_norm, e_w_ukv, e_lambda_q1, e_lambda_k1, e_lambda_q2, e_lambda_k2, e_subln, e_w_out, o_attn_norm, o_w_qkv, o_q_norm, o_k_norm, o_w_out, ffn_norm, w_gate, w_up, w_down, final_norm):
    b, s, d = x.shape
    assert (s, d) == (SEQ, D_MODEL)
    n = b * s
    log2e = math.log2(math.e)
    dcos, dsin, mcos, msin, acos, asin = _rope_tables()
    h = x.reshape(n, d)

    wbig, qn, wuq, kvn, wukv = _prep_even_weights(e_w_in[0], e_q_norm[0], e_w_uq[0], e_kv_norm[0], e_w_ukv[0])
    qm, km, vm, qd, kd, vd = _even_proj(
        h, e_attn_norm[0][None, :], wbig, qn, wuq, kvn, wukv, (dcos, dsin, mcos, msin),
        q_scale_mla=(MLA_NOPE + MLA_ROPE) ** -0.5 * log2e, q_scale_diff=DIFF_DIM ** -0.5 * log2e)
    o_mla = _attention(_mla_attn_kernel, qm, km, vm, [], q_w=2 * LANES, k_w=2 * LANES, v_w=LANES, o_w=LANES,
                       tq=ATT_ROWS, n_groups=MLA_HEADS // 2, name="mla_attn")
    lam_init = 0.8 - 0.6 * math.exp(-0.3 * 0)
    lam_args = [e_lambda_q1[0][None, :], e_lambda_k1[0][None, :], e_lambda_q2[0][None, :], e_lambda_k2[0][None, :],
                e_subln[0][None, :]]
    o_diff = _attention(functools.partial(_diff_attn_kernel, lam_init=lam_init), qd, kd, vd, lam_args,
                        q_w=LANES, k_w=LANES, v_w=LANES, o_w=LANES, tq=SHARED_ROWS // 2, n_groups=DIFF_HEADS,
                        name="diff_attn")
    w_out = e_w_out[0].astype(BF16)
    h = _post(h, [o_mla, o_diff], [w_out[:512], w_out[512:]], ffn_norm[0][None, :],
              w_gate[0].astype(BF16), w_up[0].astype(BF16), w_down[0].astype(BF16))

    wqkv, oqn, okn = _prep_odd_weights(o_w_qkv[0], o_q_norm[0], o_k_norm[0])
    q, k, v = _odd_proj(h, o_attn_norm[0][None, :], wqkv, oqn, okn, acos, asin, q_scale=LANES ** -0.5 * log2e)
    o = _attention(_gqa_attn_kernel, q, k, v, [], q_w=GQA_GROUP * LANES, k_w=LANES, v_w=LANES, o_w=GQA_GROUP * LANES,
                   tq=SHARED_ROWS // GQA_GROUP, n_groups=GQA_KV_HEADS, name="gqa_attn")
    out = _post(h, [o], [o_w_out[0].astype(BF16)], ffn_norm[1][None, :],
                w_gate[1].astype(BF16), w_up[1].astype(BF16), w_down[1].astype(BF16), final_gain=final_norm[None, :])
    return out.reshape(b, s, d)
```

```python
import functools
import math

import jax
import jax.numpy as jnp
import numpy as np
from jax import lax
from jax.experimental import pallas as pl
from jax.experimental.pallas import tpu as pltpu

F32 = jnp.float32
BF16 = jnp.bfloat16

D_MODEL = 1024
SEQ = 4096
GRID_W = 64
EPS = 1e-6
ROPE_THETA = 500000.0
AXIAL_THETA = 10000.0

MLA_HEADS = 8
MLA_Q_RANK = 192
MLA_KV_RANK = 128
MLA_NOPE = 64
MLA_ROPE = 32
MLA_V = 64
DIFF_HEADS = 4
DIFF_DIM = 64
DIFF_ROT = 16
GQA_HEADS = 8
GQA_KV_HEADS = 2
GQA_GROUP = 4
FFN_HIDDEN = 2816

LANES = 128
HALF = LANES // 2
VMEM_LIMIT = 56 * 1024 * 1024

ROW_TILE = 512
ATT_ROWS = 2048
SHARED_ROWS = 4096
KV_TILE = 256
ODD_ROW_TILE = 1024
ROW_CHUNK = 256
NEG_INIT = -1e30

DIFF_W = DIFF_HEADS * LANES
MLA_W = MLA_HEADS * LANES
CQ_W = 2 * LANES
COL_KD = DIFF_W
COL_VD = 2 * DIFF_W
COL_CQ = 3 * DIFF_W
COL_CKV = COL_CQ + CQ_W
COL_KR = COL_CKV + MLA_KV_RANK
EVEN_PROJ_W = COL_KR + LANES
GQA_QW = GQA_HEADS * LANES
GQA_KW = GQA_KV_HEADS * LANES


def _cat_slices(a, pieces):
    parts = []
    for p in pieces:
        if isinstance(p, int):
            parts.append(jnp.zeros(a.shape[:-1] + (p,), a.dtype))
        else:
            parts.append(a[..., p[0]:p[1]])
    return jnp.concatenate(parts, axis=-1)


_DIFF_PIECES = ((0, 8), (16, 40), (64, 72), (80, 104), (8, 16), (40, 64), (72, 80), (104, 128))
_MLA_QK_PIECES = ((0, 48), (64, 80), (48, 64), 32, (80, 96))
_MLA_KNOPE_PIECES = ((0, 48), 16, (48, 64), 48)
_MLA_KROPE_PIECES = (48, (0, 16), 48, (16, 32))
_AXIAL_PIECES = ((0, 32), (64, 96), (32, 64), (96, 128))


def _rope_tables():
    f32 = np.float32
    pos = np.arange(SEQ, dtype=np.int32).astype(f32)
    row = (np.arange(SEQ, dtype=np.int32) // GRID_W).astype(f32)
    col = (np.arange(SEQ, dtype=np.int32) % GRID_W).astype(f32)

    def cs(p, dim, theta):
        inv = (f32(theta) ** (-np.arange(0, dim, 2, dtype=f32) / f32(dim))).astype(f32)
        ang = (p[:, None] * inv[None, :]).astype(f32)
        return np.cos(ang).astype(f32), np.sin(ang).astype(f32)

    ones = lambda n: np.ones((SEQ, n), f32)
    zeros = lambda n: np.zeros((SEQ, n), f32)

    c, s = cs(pos, DIFF_ROT, ROPE_THETA)
    cblk = np.concatenate([c, ones(24)], axis=1)
    sblk = np.concatenate([s, zeros(24)], axis=1)
    diff_cos = np.concatenate([cblk] * 4, axis=1)
    diff_sin = np.concatenate([-sblk, -sblk, sblk, sblk], axis=1)

    c, s = cs(pos, MLA_ROPE, ROPE_THETA)
    mla_cos = np.concatenate([ones(48), c, ones(48), c], axis=1)
    mla_sin = np.concatenate([zeros(48), -s, zeros(48), s], axis=1)

    cr, sr = cs(row, HALF, AXIAL_THETA)
    cc, sc = cs(col, HALF, AXIAL_THETA)
    ax_cos = np.concatenate([cr, cc, cr, cc], axis=1)
    ax_sin = np.concatenate([-sr, -sc, sr, sc], axis=1)
    return tuple(jnp.asarray(t) for t in (diff_cos, diff_sin, mla_cos, mla_sin, ax_cos, ax_sin))


def _rms(x, gain, n):
    ms = jnp.sum(x * x, axis=-1, keepdims=True) * (1.0 / n)
    return x * lax.rsqrt(ms + EPS) * gain


def _rope(x, cos, sin):
    return x * cos + pltpu.roll(x, HALF, 1) * sin


def _dot(a, b):
    return jnp.dot(a, b, preferred_element_type=F32)


def _even_proj_kernel(x_ref, g_ref, wbig_ref, qn_ref, wuq_ref, kvn_ref, wukv_ref,
                      dcos_ref, dsin_ref, mcos_ref, msin_ref,
                      qm_ref, km_ref, vm_ref, qd_ref, kd_ref, vd_ref, *, q_scale_mla, q_scale_diff):
    xn = _rms(x_ref[...], g_ref[...], D_MODEL).astype(BF16)
    big = _dot(xn, wbig_ref[...])
    dcos, dsin = dcos_ref[...], dsin_ref[...]
    mcos, msin = mcos_ref[...], msin_ref[...]

    for h in range(DIFF_HEADS):
        sl = slice(h * LANES, (h + 1) * LANES)
        qd_ref[:, sl] = (_rope(big[:, sl], dcos, dsin) * q_scale_diff).astype(BF16)
        kd_ref[:, sl] = _rope(big[:, COL_KD + h * LANES:COL_KD + (h + 1) * LANES], dcos, dsin).astype(BF16)
    vd_ref[...] = big[:, COL_VD:COL_VD + DIFF_W].astype(BF16)

    cq = big[:, COL_CQ:COL_CQ + CQ_W]
    cqn = _rms(cq, qn_ref[...], MLA_Q_RANK).astype(BF16)
    q = _dot(cqn, wuq_ref[...])
    ckv = big[:, COL_CKV:COL_CKV + MLA_KV_RANK]
    ckvn = _rms(ckv, kvn_ref[...], MLA_KV_RANK).astype(BF16)
    kv = _dot(ckvn, wukv_ref[...])
    kr = _rope(big[:, COL_KR:COL_KR + LANES], mcos, msin)
    for h in range(MLA_HEADS):
        sl = slice(h * LANES, (h + 1) * LANES)
        qm_ref[:, sl] = (_rope(q[:, sl], mcos, msin) * q_scale_mla).astype(BF16)
        km_ref[:, sl] = (kv[:, sl] + kr).astype(BF16)
    vm_ref[...] = kv[:, MLA_W:MLA_W + MLA_HEADS * MLA_V].astype(BF16)


def _even_proj(x2, g, wbig, qn, wuq, kvn, wukv, tabs, q_scale_mla, q_scale_diff):
    n = x2.shape[0]
    tm = ROW_TILE
    ns = SEQ // tm
    row = lambda w: pl.BlockSpec((tm, w), lambda i: (i, 0))
    full = lambda a: pl.BlockSpec(a.shape, lambda i: (0, 0), pipeline_mode=pl.Buffered(1))
    tab = pl.BlockSpec((tm, LANES), lambda i: (i % ns, 0))
    dcos, dsin, mcos, msin = tabs
    out_w = (MLA_W, MLA_W, MLA_HEADS * MLA_V, DIFF_W, DIFF_W, DIFF_W)
    return pl.pallas_call(
        functools.partial(_even_proj_kernel, q_scale_mla=q_scale_mla, q_scale_diff=q_scale_diff),
        grid=(n // tm,),
        in_specs=[row(D_MODEL), full(g), full(wbig), full(qn), full(wuq), full(kvn), full(wukv), tab, tab, tab, tab],
        out_specs=[row(w) for w in out_w],
        out_shape=[jax.ShapeDtypeStruct((n, w), BF16) for w in out_w],
        compiler_params=pltpu.CompilerParams(dimension_semantics=("parallel",), vmem_limit_bytes=VMEM_LIMIT),
        name="even_proj",
    )(x2, g, wbig, qn, wuq, kvn, wukv, dcos, dsin, mcos, msin)


def _flash_rows(q, k_ref, v_ref, kcol, vcol):
    rows = q.shape[0]
    ones = jnp.ones((KV_TILE, LANES), BF16)
    m = jnp.full((rows, 1), NEG_INIT, F32)
    acc = jnp.zeros((rows, 2 * LANES), F32)
    for start in range(0, k_ref.shape[0], KV_TILE):
        ks = k_ref[start:start + KV_TILE, kcol:kcol + LANES]
        vs = jnp.concatenate([v_ref[start:start + KV_TILE, vcol:vcol + LANES], ones], axis=1)
        s = lax.dot_general(q, ks, (((1,), (1,)), ((), ())), preferred_element_type=F32)
        m_new = jnp.maximum(m, jnp.max(s, axis=-1, keepdims=True))
        alpha = jnp.exp2(m - m_new)
        p = jnp.exp2(s - m_new)
        acc = alpha * acc + _dot(p.astype(BF16), vs)
        m = m_new
    return acc[:, 0:LANES], acc[:, LANES:2 * LANES]


def _mla_attn_kernel(q_ref, k_ref, v_ref, o_ref):
    acc0, l0 = _flash_rows(q_ref[:, 0:LANES], k_ref, v_ref, 0, 0)
    acc1, l1 = _flash_rows(q_ref[:, LANES:2 * LANES], k_ref, v_ref, LANES, 0)
    lane = lax.broadcasted_iota(jnp.int32, acc0.shape, 1)
    o = jnp.where(lane < MLA_V, acc0 * (1.0 / l0), acc1 * (1.0 / l1))
    o_ref[...] = o.astype(BF16)


def _diff_attn_kernel(q_ref, k_ref, v_ref, lq1_ref, lk1_ref, lq2_ref, lk2_ref, sub_ref, o_ref, *, lam_init):
    q = q_ref[...]
    tq = q.shape[0]
    lane = lax.broadcasted_iota(jnp.int32, q.shape, 1)
    comp0 = (lane % HALF) < (HALF // 2)
    zero = jnp.zeros_like(q)
    q2 = jnp.concatenate([jnp.where(comp0, q, zero), jnp.where(comp0, zero, q)], axis=0)
    acc, l = _flash_rows(q2, k_ref, v_ref, 0, 0)
    o = acc * (1.0 / l)
    lam = (jnp.exp(jnp.sum(lq1_ref[...] * lk1_ref[...], axis=-1, keepdims=True))
           - jnp.exp(jnp.sum(lq2_ref[...] * lk2_ref[...], axis=-1, keepdims=True)) + lam_init)
    od = o[:tq] - lam * o[tq:]
    od = _rms(od, sub_ref[...], LANES) * (1.0 - lam_init)
    o_ref[...] = od.astype(BF16)


def _gqa_attn_kernel(q_ref, k_ref, v_ref, o_ref):
    tq = q_ref.shape[0]
    q = jnp.concatenate([q_ref[:, g * LANES:(g + 1) * LANES] for g in range(GQA_GROUP)], axis=0)
    acc, l = _flash_rows(q, k_ref, v_ref, 0, 0)
    o = acc * (1.0 / l)
    for g in range(GQA_GROUP):
        o_ref[:, g * LANES:(g + 1) * LANES] = o[g * tq:(g + 1) * tq].astype(BF16)


def _attention(body, q, k, v, extra, *, q_w, k_w, v_w, o_w, tq, n_groups, name):
    n = q.shape[0]
    b = n // SEQ
    nq = SEQ // tq
    in_specs = [
        pl.BlockSpec((tq, q_w), lambda bi, g, i: (bi * nq + i, g)),
        pl.BlockSpec((SEQ, k_w), lambda bi, g, i: (bi, g)),
        pl.BlockSpec((SEQ, v_w), lambda bi, g, i: (bi, g)),
    ] + [pl.BlockSpec(a.shape, lambda bi, g, i: (0, 0)) for a in extra]
    return pl.pallas_call(
        body,
        grid=(b, n_groups, nq),
        in_specs=in_specs,
        out_specs=pl.BlockSpec((tq, o_w), lambda bi, g, i: (bi * nq + i, g)),
        out_shape=jax.ShapeDtypeStruct((n, n_groups * o_w), BF16),
        compiler_params=pltpu.CompilerParams(
            dimension_semantics=("parallel", "parallel", "parallel"), vmem_limit_bytes=VMEM_LIMIT),
        name=name,
    )(q, k, v, *extra)


def _post_kernel(*refs, n_mix, final):
    h_ref = refs[0]
    o_refs = refs[1:1 + n_mix]
    w_refs = refs[1 + n_mix:1 + 2 * n_mix]
    g_ref, wg_ref, wu_ref, wd_ref = refs[1 + 2 * n_mix:5 + 2 * n_mix]
    rest = refs[5 + 2 * n_mix:]
    if final:
        fg_ref, out_ref = rest
    else:
        (out_ref,) = rest
    h = h_ref[...]
    for o_ref, w_ref in zip(o_refs, w_refs):
        h = h + _dot(o_ref[...], w_ref[...])
    hn = _rms(h, g_ref[...], D_MODEL).astype(BF16)
    gate = _dot(hn, wg_ref[...])
    up = _dot(hn, wu_ref[...])
    act = (gate * (1.0 / (1.0 + jnp.exp(-gate))) * up).astype(BF16)
    h = h + _dot(act, wd_ref[...])
    if final:
        h = _rms(h, fg_ref[...], D_MODEL)
    out_ref[...] = h


def _post(h2, mixes, w_outs, g, wg, wu, wd, final_gain=None):
    n = h2.shape[0]
    tm = ROW_TILE
    row = lambda w: pl.BlockSpec((tm, w), lambda i: (i, 0))
    const = lambda a: pl.BlockSpec(a.shape, lambda i: (0, 0), pipeline_mode=pl.Buffered(1))
    final = final_gain is not None
    args = [h2, *mixes, *w_outs, g, wg, wu, wd] + ([final_gain] if final else [])
    in_specs = ([row(D_MODEL)] + [row(m.shape[1]) for m in mixes] + [const(w) for w in w_outs]
                + [const(g), const(wg), const(wu), const(wd)] + ([const(final_gain)] if final else []))
    return pl.pallas_call(
        functools.partial(_post_kernel, n_mix=len(mixes), final=final),
        grid=(n // tm,),
        in_specs=in_specs,
        out_specs=row(D_MODEL),
        out_shape=jax.ShapeDtypeStruct((n, D_MODEL), F32),
        compiler_params=pltpu.CompilerParams(dimension_semantics=("parallel",), vmem_limit_bytes=VMEM_LIMIT),
        name="post_final" if final else "post",
    )(*args)


def _odd_proj_kernel(h_ref, g_ref, w_ref, qn_ref, kn_ref, cos_ref, sin_ref, q_ref, k_ref, v_ref, *, q_scale):
    qn, kn = qn_ref[...], kn_ref[...]
    for r0 in range(0, h_ref.shape[0], ROW_CHUNK):
        rows = slice(r0, r0 + ROW_CHUNK)
        hn = _rms(h_ref[rows, :], g_ref[...], D_MODEL).astype(BF16)
        proj = _dot(hn, w_ref[...])
        cos, sin = cos_ref[rows, :], sin_ref[rows, :]
        for h in range(GQA_HEADS):
            sl = slice(h * LANES, (h + 1) * LANES)
            q_ref[rows, sl] = (_rope(_rms(proj[:, sl], qn, LANES), cos, sin) * q_scale).astype(BF16)
        for h in range(GQA_KV_HEADS):
            sl = slice(h * LANES, (h + 1) * LANES)
            kh = proj[:, GQA_QW + h * LANES:GQA_QW + (h + 1) * LANES]
            k_ref[rows, sl] = _rope(_rms(kh, kn, LANES), cos, sin).astype(BF16)
        v_ref[rows, :] = proj[:, GQA_QW + GQA_KW:GQA_QW + 2 * GQA_KW].astype(BF16)


def _odd_proj(h2, g, w, qn, kn, cos, sin, q_scale):
    n = h2.shape[0]
    tm = ODD_ROW_TILE
    ns = SEQ // tm
    row = lambda w_: pl.BlockSpec((tm, w_), lambda i: (i, 0))
    full = lambda a: pl.BlockSpec(a.shape, lambda i: (0, 0), pipeline_mode=pl.Buffered(1))
    tab = pl.BlockSpec((tm, LANES), lambda i: (i % ns, 0))
    out_w = (GQA_QW, GQA_KW, GQA_KW)
    return pl.pallas_call(
        functools.partial(_odd_proj_kernel, q_scale=q_scale),
        grid=(n // tm,),
        in_specs=[row(D_MODEL), full(g), full(w), full(qn), full(kn), tab, tab],
        out_specs=[row(w_) for w_ in out_w],
        out_shape=[jax.ShapeDtypeStruct((n, w_), BF16) for w_ in out_w],
        compiler_params=pltpu.CompilerParams(dimension_semantics=("parallel",), vmem_limit_bytes=VMEM_LIMIT),
        name="odd_proj",
    )(h2, g, w, qn, kn, cos, sin)


def _prep_even_weights(w_in, q_norm, w_uq, kv_norm, w_ukv):
    o_kv = MLA_Q_RANK
    o_kr = o_kv + MLA_KV_RANK
    o_d = o_kr + MLA_ROPE
    w_cq, w_ckv, w_kr = w_in[:, :o_kv], w_in[:, o_kv:o_kr], w_in[:, o_kr:o_d]
    w_qd, w_kd, w_vd = (w_in[:, o_d + i * DIFF_W:o_d + (i + 1) * DIFF_W] for i in range(3))
    slab = lambda w: _cat_slices(w.reshape(D_MODEL, DIFF_HEADS, LANES), _DIFF_PIECES).reshape(D_MODEL, DIFF_W)
    wbig = jnp.concatenate(
        [slab(w_qd), slab(w_kd), w_vd, w_cq, jnp.zeros((D_MODEL, CQ_W - MLA_Q_RANK), F32), w_ckv,
         _cat_slices(w_kr, _MLA_KROPE_PIECES)], axis=1).astype(BF16)
    assert wbig.shape[1] == EVEN_PROJ_W

    qn = jnp.concatenate([q_norm, jnp.zeros((CQ_W - MLA_Q_RANK,), F32)])[None, :]
    uq = _cat_slices(w_uq.reshape(MLA_Q_RANK, MLA_HEADS, MLA_NOPE + MLA_ROPE), _MLA_QK_PIECES)
    wuq = jnp.concatenate([uq.reshape(MLA_Q_RANK, MLA_W),
                           jnp.zeros((CQ_W - MLA_Q_RANK, MLA_W), F32)], axis=0).astype(BF16)

    ukv = w_ukv.reshape(MLA_KV_RANK, MLA_HEADS, MLA_NOPE + MLA_V)
    wk = _cat_slices(ukv[:, :, :MLA_NOPE], _MLA_KNOPE_PIECES)
    wv = ukv[:, :, MLA_NOPE:]
    wukv = jnp.concatenate([wk.reshape(MLA_KV_RANK, -1), wv.reshape(MLA_KV_RANK, -1)], axis=1).astype(BF16)
    return wbig, qn, wuq, kv_norm[None, :], wukv


def _prep_odd_weights(w_qkv, q_norm, k_norm):
    nqk = GQA_HEADS + GQA_KV_HEADS
    qk = _cat_slices(w_qkv[:, :nqk * LANES].reshape(D_MODEL, nqk, LANES), _AXIAL_PIECES).reshape(D_MODEL, nqk * LANES)
    w = jnp.concatenate([qk, w_qkv[:, nqk * LANES:]], axis=1).astype(BF16)
    return w, _cat_slices(q_norm, _AXIAL_PIECES)[None, :], _cat_slices(k_norm, _AXIAL_PIECES)[None, :]


def kernel(x, e_attn_norm, e_w_in, e_q_norm, e_w_uq, e_kv_norm, e_w_ukv, e_lambda_q1, e_lambda_k1, e_lambda_q2, e_lambda_k2, e_subln, e_w_out, o_attn_norm, o_w_qkv, o_q_norm, o_k_norm, o_w_out, ffn_norm, w_gate, w_up, w_down, final_norm):
    b, s, d = x.shape
    assert (s, d) == (SEQ, D_MODEL)
    n = b * s
    log2e = math.log2(math.e)
    dcos, dsin, mcos, msin, acos, asin = _rope_tables()
    h = x.reshape(n, d)

    wbig, qn, wuq, kvn, wukv = _prep_even_weights(e_w_in[0], e_q_norm[0], e_w_uq[0], e_kv_norm[0], e_w_ukv[0])
    qm, km, vm, qd, kd, vd = _even_proj(
        h, e_attn_norm[0][None, :], wbig, qn, wuq, kvn, wukv, (dcos, dsin, mcos, msin),
        q_scale_mla=(MLA_NOPE + MLA_ROPE) ** -0.5 * log2e, q_scale_diff=DIFF_DIM ** -0.5 * log2e)
    o_mla = _attention(_mla_attn_kernel, qm, km, vm, [], q_w=2 * LANES, k_w=2 * LANES, v_w=LANES, o_w=LANES,
                       tq=ATT_ROWS, n_groups=MLA_HEADS // 2, name="mla_attn")
    lam_init = 0.8 - 0.6 * math.exp(-0.3 * 0)
    lam_args = [e_lambda_q1[0][None, :], e_lambda_k1[0][None, :], e_lambda_q2[0][None, :], e_lambda_k2[0][None, :],
                e_subln[0][None, :]]
    o_diff = _attention(functools.partial(_diff_attn_kernel, lam_init=lam_init), qd, kd, vd, lam_args,
                        q_w=LANES, k_w=LANES, v_w=LANES, o_w=LANES, tq=SHARED_ROWS // 2, n_groups=DIFF_HEADS,
                        name="diff_attn")
    w_out = e_w_out[0].astype(BF16)
    n_mla = MLA_HEADS * MLA_V
    h = _post(h, [o_mla, o_diff], [w_out[:n_mla], w_out[n_mla:]], ffn_norm[0][None, :],
              w_gate[0].astype(BF16), w_up[0].astype(BF16), w_down[0].astype(BF16))

    wqkv, oqn, okn = _prep_odd_weights(o_w_qkv[0], o_q_norm[0], o_k_norm[0])
    q, k, v = _odd_proj(h, o_attn_norm[0][None, :], wqkv, oqn, okn, acos, asin, q_scale=LANES ** -0.5 * log2e)
    o = _attention(_gqa_attn_kernel, q, k, v, [], q_w=GQA_GROUP * LANES, k_w=LANES, v_w=LANES, o_w=GQA_GROUP * LANES,
                   tq=SHARED_ROWS // GQA_GROUP, n_groups=GQA_KV_HEADS, name="gqa_attn")
    out = _post(h, [o], [o_w_out[0].astype(BF16)], ffn_norm[1][None, :],
                w_gate[1].astype(BF16), w_up[1].astype(BF16), w_down[1].astype(BF16), final_gain=final_norm[None, :])
    return out.reshape(b, s, d)
```

```python
import functools
import math

import jax
import jax.numpy as jnp
import numpy as np
from jax import lax
from jax.experimental import pallas as pl
from jax.experimental.pallas import tpu as pltpu

F32 = jnp.float32
BF16 = jnp.bfloat16

D_MODEL = 1024
SEQ = 4096
GRID_W = 64
EPS = 1e-6
ROPE_THETA = 500000.0
AXIAL_THETA = 10000.0

MLA_HEADS = 8
MLA_Q_RANK = 192
MLA_KV_RANK = 128
MLA_NOPE = 64
MLA_ROPE = 32
MLA_V = 64
DIFF_HEADS = 4
DIFF_DIM = 64
DIFF_ROT = 16
GQA_HEADS = 8
GQA_KV_HEADS = 2
GQA_GROUP = 4
FFN_HIDDEN = 2816

LANES = 128
HALF = LANES // 2
VMEM_LIMIT = 56 * 1024 * 1024

ROW_TILE = 512
ATT_ROWS = 2048
SHARED_ROWS = 4096
KV_TILE = 256
ODD_ROW_TILE = 1024
ROW_CHUNK = 256
NEG_INIT = -1e30

DIFF_W = DIFF_HEADS * LANES
MLA_W = MLA_HEADS * LANES
CQ_W = 2 * LANES
COL_KD = DIFF_W
COL_VD = 2 * DIFF_W
COL_CQ = 3 * DIFF_W
COL_CKV = COL_CQ + CQ_W
COL_KR = COL_CKV + MLA_KV_RANK
EVEN_PROJ_W = COL_KR + LANES
GQA_QW = GQA_HEADS * LANES
GQA_KW = GQA_KV_HEADS * LANES


def _cat_slices(a, pieces):
    parts = []
    for p in pieces:
        if isinstance(p, int):
            parts.append(jnp.zeros(a.shape[:-1] + (p,), a.dtype))
        else:
            parts.append(a[..., p[0]:p[1]])
    return jnp.concatenate(parts, axis=-1)


_DIFF_PIECES = ((0, 8), (16, 40), (64, 72), (80, 104), (8, 16), (40, 64), (72, 80), (104, 128))
_MLA_QK_PIECES = ((0, 48), (64, 80), (48, 64), 32, (80, 96))
_MLA_KNOPE_PIECES = ((0, 48), 16, (48, 64), 48)
_MLA_KROPE_PIECES = (48, (0, 16), 48, (16, 32))
_AXIAL_PIECES = ((0, 32), (64, 96), (32, 64), (96, 128))


def _rope_tables():
    f32 = np.float32
    pos = np.arange(SEQ, dtype=np.int32).astype(f32)
    row = (np.arange(SEQ, dtype=np.int32) // GRID_W).astype(f32)
    col = (np.arange(SEQ, dtype=np.int32) % GRID_W).astype(f32)

    def cs(p, dim, theta):
        inv = (f32(theta) ** (-np.arange(0, dim, 2, dtype=f32) / f32(dim))).astype(f32)
        ang = (p[:, None] * inv[None, :]).astype(f32)
        return np.cos(ang).astype(f32), np.sin(ang).astype(f32)

    ones = lambda n: np.ones((SEQ, n), f32)
    zeros = lambda n: np.zeros((SEQ, n), f32)

    c, s = cs(pos, DIFF_ROT, ROPE_THETA)
    cblk = np.concatenate([c, ones(24)], axis=1)
    sblk = np.concatenate([s, zeros(24)], axis=1)
    diff_cos = np.concatenate([cblk] * 4, axis=1)
    diff_sin = np.concatenate([-sblk, -sblk, sblk, sblk], axis=1)

    c, s = cs(pos, MLA_ROPE, ROPE_THETA)
    mla_cos = np.concatenate([ones(48), c, ones(48), c], axis=1)
    mla_sin = np.concatenate([zeros(48), -s, zeros(48), s], axis=1)

    cr, sr = cs(row, HALF, AXIAL_THETA)
    cc, sc = cs(col, HALF, AXIAL_THETA)
    ax_cos = np.concatenate([cr, cc, cr, cc], axis=1)
    ax_sin = np.concatenate([-sr, -sc, sr, sc], axis=1)
    return tuple(jnp.asarray(t) for t in (diff_cos, diff_sin, mla_cos, mla_sin, ax_cos, ax_sin))


def _rms(x, gain, n):
    ms = jnp.sum(x * x, axis=-1, keepdims=True) * (1.0 / n)
    return x * lax.rsqrt(ms + EPS) * gain


def _rope(x, cos, sin):
    return x * cos + pltpu.roll(x, HALF, 1) * sin


def _dot(a, b):
    return jnp.dot(a, b, preferred_element_type=F32)


def _even_proj_kernel(x_ref, g_ref, wbig_ref, qn_ref, wuq_ref, kvn_ref, wukv_ref,
                      dcos_ref, dsin_ref, mcos_ref, msin_ref,
                      qm_ref, km_ref, vm_ref, qd_ref, kd_ref, vd_ref, *, q_scale_mla, q_scale_diff):
    xn = _rms(x_ref[...], g_ref[...], D_MODEL).astype(BF16)
    big = _dot(xn, wbig_ref[...])
    dcos, dsin = dcos_ref[...], dsin_ref[...]
    mcos, msin = mcos_ref[...], msin_ref[...]

    for h in range(DIFF_HEADS):
        sl = slice(h * LANES, (h + 1) * LANES)
        qd_ref[:, sl] = (_rope(big[:, sl], dcos, dsin) * q_scale_diff).astype(BF16)
        kd_ref[:, sl] = _rope(big[:, COL_KD + h * LANES:COL_KD + (h + 1) * LANES], dcos, dsin).astype(BF16)
    vd_ref[...] = big[:, COL_VD:COL_VD + DIFF_W].astype(BF16)

    cq = big[:, COL_CQ:COL_CQ + CQ_W]
    cqn = _rms(cq, qn_ref[...], MLA_Q_RANK).astype(BF16)
    q = _dot(cqn, wuq_ref[...])
    ckv = big[:, COL_CKV:COL_CKV + MLA_KV_RANK]
    ckvn = _rms(ckv, kvn_ref[...], MLA_KV_RANK).astype(BF16)
    kv = _dot(ckvn, wukv_ref[...])
    kr = _rope(big[:, COL_KR:COL_KR + LANES], mcos, msin)
    for h in range(MLA_HEADS):
        sl = slice(h * LANES, (h + 1) * LANES)
        qm_ref[:, sl] = (_rope(q[:, sl], mcos, msin) * q_scale_mla).astype(BF16)
        km_ref[:, sl] = (kv[:, sl] + kr).astype(BF16)
    vm_ref[...] = kv[:, MLA_W:MLA_W + MLA_HEADS * MLA_V].astype(BF16)


def _even_proj(x2, g, wbig, qn, wuq, kvn, wukv, tabs, q_scale_mla, q_scale_diff):
    n = x2.shape[0]
    tm = ROW_TILE
    ns = SEQ // tm
    row = lambda w: pl.BlockSpec((tm, w), lambda i: (i, 0))
    full = lambda a: pl.BlockSpec(a.shape, lambda i: (0, 0), pipeline_mode=pl.Buffered(1))
    tab = pl.BlockSpec((tm, LANES), lambda i: (i % ns, 0))
    dcos, dsin, mcos, msin = tabs
    out_w = (MLA_W, MLA_W, MLA_HEADS * MLA_V, DIFF_W, DIFF_W, DIFF_W)
    return pl.pallas_call(
        functools.partial(_even_proj_kernel, q_scale_mla=q_scale_mla, q_scale_diff=q_scale_diff),
        grid=(n // tm,),
        in_specs=[row(D_MODEL), full(g), full(wbig), full(qn), full(wuq), full(kvn), full(wukv), tab, tab, tab, tab],
        out_specs=[row(w) for w in out_w],
        out_shape=[jax.ShapeDtypeStruct((n, w), BF16) for w in out_w],
        compiler_params=pltpu.CompilerParams(dimension_semantics=("parallel",), vmem_limit_bytes=VMEM_LIMIT),
        name="even_proj",
    )(x2, g, wbig, qn, wuq, kvn, wukv, dcos, dsin, mcos, msin)


def _flash_rows(q, k_ref, v_ref, kcol, vcol):
    ones = jnp.ones((KV_TILE, LANES), BF16)
    m = acc = None
    for start in range(0, k_ref.shape[0], KV_TILE):
        ks = k_ref[start:start + KV_TILE, kcol:kcol + LANES]
        vs = jnp.concatenate([v_ref[start:start + KV_TILE, vcol:vcol + LANES], ones], axis=1)
        s = lax.dot_general(q, ks, (((1,), (1,)), ((), ())), preferred_element_type=F32)
        tile_max = jnp.max(s, axis=-1, keepdims=True)
        if start == 0:
            m = tile_max
            acc = _dot(jnp.exp2(s - m).astype(BF16), vs)
        else:
            m_new = jnp.maximum(m, tile_max)
            alpha = jnp.exp2(m - m_new)
            acc = alpha * acc + _dot(jnp.exp2(s - m_new).astype(BF16), vs)
            m = m_new
    return acc[:, 0:LANES], acc[:, LANES:2 * LANES]


def _mla_attn_kernel(q_ref, k_ref, v_ref, o_ref):
    acc0, l0 = _flash_rows(q_ref[:, 0:LANES], k_ref, v_ref, 0, 0)
    acc1, l1 = _flash_rows(q_ref[:, LANES:2 * LANES], k_ref, v_ref, LANES, 0)
    lane = lax.broadcasted_iota(jnp.int32, acc0.shape, 1)
    o = jnp.where(lane < MLA_V, acc0 * (1.0 / l0), acc1 * (1.0 / l1))
    o_ref[...] = o.astype(BF16)


def _diff_attn_kernel(q_ref, k_ref, v_ref, lq1_ref, lk1_ref, lq2_ref, lk2_ref, sub_ref, o_ref, *, lam_init):
    q = q_ref[...]
    tq = q.shape[0]
    lane = lax.broadcasted_iota(jnp.int32, q.shape, 1)
    comp0 = (lane % HALF) < (HALF // 2)
    zero = jnp.zeros_like(q)
    q2 = jnp.concatenate([jnp.where(comp0, q, zero), jnp.where(comp0, zero, q)], axis=0)
    acc, l = _flash_rows(q2, k_ref, v_ref, 0, 0)
    o = acc * (1.0 / l)
    lam = (jnp.exp(jnp.sum(lq1_ref[...] * lk1_ref[...], axis=-1, keepdims=True))
           - jnp.exp(jnp.sum(lq2_ref[...] * lk2_ref[...], axis=-1, keepdims=True)) + lam_init)
    od = o[:tq] - lam * o[tq:]
    od = _rms(od, sub_ref[...], LANES) * (1.0 - lam_init)
    o_ref[...] = od.astype(BF16)


def _gqa_attn_kernel(q_ref, k_ref, v_ref, o_ref):
    tq = q_ref.shape[0]
    q = jnp.concatenate([q_ref[:, g * LANES:(g + 1) * LANES] for g in range(GQA_GROUP)], axis=0)
    acc, l = _flash_rows(q, k_ref, v_ref, 0, 0)
    o = acc * (1.0 / l)
    for g in range(GQA_GROUP):
        o_ref[:, g * LANES:(g + 1) * LANES] = o[g * tq:(g + 1) * tq].astype(BF16)


def _attention(body, q, k, v, extra, *, q_w, k_w, v_w, o_w, tq, n_groups, name):
    n = q.shape[0]
    b = n // SEQ
    nq = SEQ // tq
    in_specs = [
        pl.BlockSpec((tq, q_w), lambda bi, g, i: (bi * nq + i, g)),
        pl.BlockSpec((SEQ, k_w), lambda bi, g, i: (bi, g)),
        pl.BlockSpec((SEQ, v_w), lambda bi, g, i: (bi, g)),
    ] + [pl.BlockSpec(a.shape, lambda bi, g, i: (0, 0)) for a in extra]
    return pl.pallas_call(
        body,
        grid=(b, n_groups, nq),
        in_specs=in_specs,
        out_specs=pl.BlockSpec((tq, o_w), lambda bi, g, i: (bi * nq + i, g)),
        out_shape=jax.ShapeDtypeStruct((n, n_groups * o_w), BF16),
        compiler_params=pltpu.CompilerParams(
            dimension_semantics=("parallel", "parallel", "parallel"), vmem_limit_bytes=VMEM_LIMIT),
        name=name,
    )(q, k, v, *extra)


def _post_kernel(*refs, n_mix, final):
    h_ref = refs[0]
    o_refs = refs[1:1 + n_mix]
    w_refs = refs[1 + n_mix:1 + 2 * n_mix]
    g_ref, wg_ref, wu_ref, wd_ref = refs[1 + 2 * n_mix:5 + 2 * n_mix]
    rest = refs[5 + 2 * n_mix:]
    if final:
        fg_ref, out_ref = rest
    else:
        (out_ref,) = rest
    h = h_ref[...]
    for o_ref, w_ref in zip(o_refs, w_refs):
        h = h + _dot(o_ref[...], w_ref[...])
    hn = _rms(h, g_ref[...], D_MODEL).astype(BF16)
    gate = _dot(hn, wg_ref[...])
    up = _dot(hn, wu_ref[...])
    act = (gate * (1.0 / (1.0 + jnp.exp(-gate))) * up).astype(BF16)
    h = h + _dot(act, wd_ref[...])
    if final:
        h = _rms(h, fg_ref[...], D_MODEL)
    out_ref[...] = h


def _post(h2, mixes, w_outs, g, wg, wu, wd, final_gain=None):
    n = h2.shape[0]
    tm = ROW_TILE
    row = lambda w: pl.BlockSpec((tm, w), lambda i: (i, 0))
    const = lambda a: pl.BlockSpec(a.shape, lambda i: (0, 0), pipeline_mode=pl.Buffered(1))
    final = final_gain is not None
    args = [h2, *mixes, *w_outs, g, wg, wu, wd] + ([final_gain] if final else [])
    in_specs = ([row(D_MODEL)] + [row(m.shape[1]) for m in mixes] + [const(w) for w in w_outs]
                + [const(g), const(wg), const(wu), const(wd)] + ([const(final_gain)] if final else []))
    return pl.pallas_call(
        functools.partial(_post_kernel, n_mix=len(mixes), final=final),
        grid=(n // tm,),
        in_specs=in_specs,
        out_specs=row(D_MODEL),
        out_shape=jax.ShapeDtypeStruct((n, D_MODEL), F32),
        compiler_params=pltpu.CompilerParams(dimension_semantics=("parallel",), vmem_limit_bytes=VMEM_LIMIT),
        name="post_final" if final else "post",
    )(*args)


def _odd_proj_kernel(h_ref, g_ref, w_ref, qn_ref, kn_ref, cos_ref, sin_ref, q_ref, k_ref, v_ref, *, q_scale):
    qn, kn = qn_ref[...], kn_ref[...]
    for r0 in range(0, h_ref.shape[0], ROW_CHUNK):
        rows = slice(r0, r0 + ROW_CHUNK)
        hn = _rms(h_ref[rows, :], g_ref[...], D_MODEL).astype(BF16)
        proj = _dot(hn, w_ref[...])
        cos, sin = cos_ref[rows, :], sin_ref[rows, :]
        for h in range(GQA_HEADS):
            sl = slice(h * LANES, (h + 1) * LANES)
            q_ref[rows, sl] = (_rope(_rms(proj[:, sl], qn, LANES), cos, sin) * q_scale).astype(BF16)
        for h in range(GQA_KV_HEADS):
            sl = slice(h * LANES, (h + 1) * LANES)
            kh = proj[:, GQA_QW + h * LANES:GQA_QW + (h + 1) * LANES]
            k_ref[rows, sl] = _rope(_rms(kh, kn, LANES), cos, sin).astype(BF16)
        v_ref[rows, :] = proj[:, GQA_QW + GQA_KW:GQA_QW + 2 * GQA_KW].astype(BF16)


def _odd_proj(h2, g, w, qn, kn, cos, sin, q_scale):
    n = h2.shape[0]
    tm = ODD_ROW_TILE
    ns = SEQ // tm
    row = lambda w_: pl.BlockSpec((tm, w_), lambda i: (i, 0))
    full = lambda a: pl.BlockSpec(a.shape, lambda i: (0, 0), pipeline_mode=pl.Buffered(1))
    tab = pl.BlockSpec((tm, LANES), lambda i: (i % ns, 0))
    out_w = (GQA_QW, GQA_KW, GQA_KW)
    return pl.pallas_call(
        functools.partial(_odd_proj_kernel, q_scale=q_scale),
        grid=(n // tm,),
        in_specs=[row(D_MODEL), full(g), full(w), full(qn), full(kn), tab, tab],
        out_specs=[row(w_) for w_ in out_w],
        out_shape=[jax.ShapeDtypeStruct((n, w_), BF16) for w_ in out_w],
        compiler_params=pltpu.CompilerParams(dimension_semantics=("parallel",), vmem_limit_bytes=VMEM_LIMIT),
        name="odd_proj",
    )(h2, g, w, qn, kn, cos, sin)


def _prep_even_weights(w_in, q_norm, w_uq, kv_norm, w_ukv):
    o_kv = MLA_Q_RANK
    o_kr = o_kv + MLA_KV_RANK
    o_d = o_kr + MLA_ROPE
    w_cq, w_ckv, w_kr = w_in[:, :o_kv], w_in[:, o_kv:o_kr], w_in[:, o_kr:o_d]
    w_qd, w_kd, w_vd = (w_in[:, o_d + i * DIFF_W:o_d + (i + 1) * DIFF_W] for i in range(3))
    slab = lambda w: _cat_slices(w.reshape(D_MODEL, DIFF_HEADS, LANES), _DIFF_PIECES).reshape(D_MODEL, DIFF_W)
    wbig = jnp.concatenate(
        [slab(w_qd), slab(w_kd), w_vd, w_cq, jnp.zeros((D_MODEL, CQ_W - MLA_Q_RANK), F32), w_ckv,
         _cat_slices(w_kr, _MLA_KROPE_PIECES)], axis=1).astype(BF16)
    assert wbig.shape[1] == EVEN_PROJ_W

    qn = jnp.concatenate([q_norm, jnp.zeros((CQ_W - MLA_Q_RANK,), F32)])[None, :]
    uq = _cat_slices(w_uq.reshape(MLA_Q_RANK, MLA_HEADS, MLA_NOPE + MLA_ROPE), _MLA_QK_PIECES)
    wuq = jnp.concatenate([uq.reshape(MLA_Q_RANK, MLA_W),
                           jnp.zeros((CQ_W - MLA_Q_RANK, MLA_W), F32)], axis=0).astype(BF16)

    ukv = w_ukv.reshape(MLA_KV_RANK, MLA_HEADS, MLA_NOPE + MLA_V)
    wk = _cat_slices(ukv[:, :, :MLA_NOPE], _MLA_KNOPE_PIECES)
    wv = ukv[:, :, MLA_NOPE:]
    wukv = jnp.concatenate([wk.reshape(MLA_KV_RANK, -1), wv.reshape(MLA_KV_RANK, -1)], axis=1).astype(BF16)
    return wbig, qn, wuq, kv_norm[None, :], wukv


def _prep_odd_weights(w_qkv, q_norm, k_norm):
    nqk = GQA_HEADS + GQA_KV_HEADS
    qk = _cat_slices(w_qkv[:, :nqk * LANES].reshape(D_MODEL, nqk, LANES), _AXIAL_PIECES).reshape(D_MODEL, nqk * LANES)
    w = jnp.concatenate([qk, w_qkv[:, nqk * LANES:]], axis=1).astype(BF16)
    return w, _cat_slices(q_norm, _AXIAL_PIECES)[None, :], _cat_slices(k_norm, _AXIAL_PIECES)[None, :]


def kernel(x, e_attn_norm, e_w_in, e_q_norm, e_w_uq, e_kv_norm, e_w_ukv, e_lambda_q1, e_lambda_k1, e_lambda_q2, e_lambda_k2, e_subln, e_w_out, o_attn_norm, o_w_qkv, o_q_norm, o_k_norm, o_w_out, ffn_norm, w_gate, w_up, w_down, final_norm):
    b, s, d = x.shape
    assert (s, d) == (SEQ, D_MODEL)
    n = b * s
    log2e = math.log2(math.e)
    dcos, dsin, mcos, msin, acos, asin = _rope_tables()
    h = x.reshape(n, d)

    wbig, qn, wuq, kvn, wukv = _prep_even_weights(e_w_in[0], e_q_norm[0], e_w_uq[0], e_kv_norm[0], e_w_ukv[0])
    qm, km, vm, qd, kd, vd = _even_proj(
        h, e_attn_norm[0][None, :], wbig, qn, wuq, kvn, wukv, (dcos, dsin, mcos, msin),
        q_scale_mla=(MLA_NOPE + MLA_ROPE) ** -0.5 * log2e, q_scale_diff=DIFF_DIM ** -0.5 * log2e)
    o_mla = _attention(_mla_attn_kernel, qm, km, vm, [], q_w=2 * LANES, k_w=2 * LANES, v_w=LANES, o_w=LANES,
                       tq=ATT_ROWS, n_groups=MLA_HEADS // 2, name="mla_attn")
    lam_init = 0.8 - 0.6 * math.exp(-0.3 * 0)
    lam_args = [e_lambda_q1[0][None, :], e_lambda_k1[0][None, :], e_lambda_q2[0][None, :], e_lambda_k2[0][None, :],
                e_subln[0][None, :]]
    o_diff = _attention(functools.partial(_diff_attn_kernel, lam_init=lam_init), qd, kd, vd, lam_args,
                        q_w=LANES, k_w=LANES, v_w=LANES, o_w=LANES, tq=SHARED_ROWS // 2, n_groups=DIFF_HEADS,
                        name="diff_attn")
    w_out = e_w_out[0].astype(BF16)
    n_mla = MLA_HEADS * MLA_V
    h = _post(h, [o_mla, o_diff], [w_out[:n_mla], w_out[n_mla:]], ffn_norm[0][None, :],
              w_gate[0].astype(BF16), w_up[0].astype(BF16), w_down[0].astype(BF16))

    wqkv, oqn, okn = _prep_odd_weights(o_w_qkv[0], o_q_norm[0], o_k_norm[0])
    q, k, v = _odd_proj(h, o_attn_norm[0][None, :], wqkv, oqn, okn, acos, asin, q_scale=LANES ** -0.5 * log2e)
    o = _attention(_gqa_attn_kernel, q, k, v, [], q_w=GQA_GROUP * LANES, k_w=LANES, v_w=LANES, o_w=GQA_GROUP * LANES,
                   tq=SHARED_ROWS // GQA_GROUP, n_groups=GQA_KV_HEADS, name="gqa_attn")
    out = _post(h, [o], [o_w_out[0].astype(BF16)], ffn_norm[1][None, :],
                w_gate[1].astype(BF16), w_up[1].astype(BF16), w_down[1].astype(BF16), final_gain=final_norm[None, :])
    return out.reshape(b, s, d)
```
